```python
import jax, jax.numpy as jnp
from jax import lax
import numpy as np


D_MODEL = 1024
BATCH = 8
SEQ = 4096
DEPTH = 4

ATT_HEAD_DIM = 64
D_ATT = D_MODEL // 2
N_ATT_HEADS = D_ATT // ATT_HEAD_DIM
D_CONV = D_MODEL // 2
N_CONV_GROUPS = 8
CONV_WIDTH = 31
Q_BLOCK = 128
LN_EPS = 1e-5
GATE_INIT_STD = 0.02
DEEPNORM_ALPHA = (2 * DEPTH) ** 0.25
DEEPNORM_BETA = (8 * DEPTH) ** -0.25
IN_WIDTHS = (D_ATT, D_ATT, D_ATT, D_ATT, D_CONV, D_CONV, D_CONV, D_MODEL, D_MODEL)
D_IN = sum(IN_WIDTHS)

kernel_name = 'stickbreak_conformer_gated_hybrid'


def _split_points():
    pts, acc = [], 0
    for w in IN_WIDTHS[:-1]:
        acc += w
        pts.append(acc)
    return pts


def layer_norm(x, g, b):
    xf = x.astype(jnp.float32)
    mu = jnp.mean(xf, axis=-1, keepdims=True)
    var = jnp.mean(jnp.square(xf - mu), axis=-1, keepdims=True)
    return ((xf - mu) * lax.rsqrt(var + LN_EPS) * g + b).astype(x.dtype)


def stick_breaking_attention(q, k, v):
    S = q.shape[1]
    scale = ATT_HEAD_DIM ** -0.5
    outs = []
    for blk in range(S // Q_BLOCK):
        q0 = blk * Q_BLOCK
        q1 = q0 + Q_BLOCK
        qb = q[:, q0:q1]
        kb = k[:, :q1]
        vb = v[:, :q1]
        z = jnp.einsum('bthd,bshd->bhts', qb, kb).astype(jnp.float32) * scale
        t_pos = q0 + jnp.arange(Q_BLOCK)[:, None]
        s_pos = jnp.arange(q1)[None, :]
        causal = s_pos < t_pos
        log_fail = jnp.where(causal, jax.nn.log_sigmoid(-z), 0.0)
        later = lax.cumsum(log_fail, axis=3, reverse=True) - log_fail
        w = jnp.where(causal, jnp.exp(jax.nn.log_sigmoid(z) + later), 0.0)
        outs.append(jnp.einsum('bhts,bshd->bthd', w.astype(vb.dtype), vb))
    return jnp.concatenate(outs, axis=1)


def causal_depthwise_conv(u, w, b):
    C = u.shape[-1]
    up = jnp.pad(u, ((0, 0), (CONV_WIDTH - 1, 0), (0, 0)))
    out = lax.conv_general_dilated(up, w[:, None, :].astype(u.dtype), window_strides=(1,), padding='VALID',
                                   dimension_numbers=('NWC', 'WIO', 'NWC'), feature_group_count=C)
    return out + b


def hybrid_layer(x, w_in, b_in, conv_w, conv_b, conv_ln_g, conv_ln_b,
                 w_att_proj, w_conv_proj, b_conv_proj, w_out, ln_g, ln_b):
    B, S, _ = x.shape
    u = jnp.einsum('bsd,de->bse', x, w_in) + b_in
    q, k, v, z_att, glu_a, glu_b, z_conv, g_att, g_conv = jnp.split(u, _split_points(), axis=-1)

    heads = lambda t: t.reshape(B, S, N_ATT_HEADS, ATT_HEAD_DIM)
    att = stick_breaking_attention(heads(q), heads(k), heads(v)).reshape(B, S, D_ATT)
    att_branch = jnp.einsum('bsc,cd->bsd', att * jax.nn.silu(z_att), w_att_proj)

    c = glu_a * jax.nn.sigmoid(glu_b)
    c = causal_depthwise_conv(c, conv_w, conv_b)
    c = jax.nn.silu(layer_norm(c, conv_ln_g, conv_ln_b))
    conv_branch = jnp.einsum('bsc,cd->bsd', c * jax.nn.silu(z_conv), w_conv_proj) + b_conv_proj

    merged = jax.nn.sigmoid(g_att) * att_branch + jax.nn.sigmoid(g_conv) * conv_branch
    y = jnp.einsum('bsd,de->bse', merged, w_out)
    return layer_norm(DEEPNORM_ALPHA * x + y, ln_g, ln_b)


def setup_inputs(seed: int = 0) -> dict:
    key = jax.random.key(seed)
    ks = jax.random.split(key, 13)
    L, D = DEPTH, D_MODEL
    nrm = lambda k, shape, s: jax.random.normal(k, shape, jnp.float32) * s
    x = nrm(ks[0], (BATCH, SEQ, D), 1.0)
    col_scale = jnp.ones((D_IN,), jnp.float32).at[2 * D_ATT:3 * D_ATT].set(DEEPNORM_BETA)
    w_in = nrm(ks[1], (L, D, D_IN), D ** -0.5) * col_scale
    b_in = nrm(ks[2], (L, D_IN), GATE_INIT_STD)
    conv_w = nrm(ks[3], (L, CONV_WIDTH, D_CONV), CONV_WIDTH ** -0.5)
    conv_b = nrm(ks[4], (L, D_CONV), GATE_INIT_STD)
    conv_ln_g = 1.0 + nrm(ks[5], (L, D_CONV), GATE_INIT_STD)
    conv_ln_b = nrm(ks[6], (L, D_CONV), GATE_INIT_STD)
    w_att_proj = nrm(ks[7], (L, D_ATT, D), D_ATT ** -0.5 * DEEPNORM_BETA)
    w_conv_proj = nrm(ks[8], (L, D_CONV, D), D_CONV ** -0.5 * DEEPNORM_BETA)
    b_conv_proj = nrm(ks[9], (L, D), GATE_INIT_STD)
    w_out = nrm(ks[10], (L, D, D), D ** -0.5 * DEEPNORM_BETA)
    ln_g = 1.0 + nrm(ks[11], (L, D), GATE_INIT_STD)
    ln_b = nrm(ks[12], (L, D), GATE_INIT_STD)
    return {'x': x, 'w_in': w_in, 'b_in': b_in, 'conv_w': conv_w, 'conv_b': conv_b,
            'conv_ln_g': conv_ln_g, 'conv_ln_b': conv_ln_b, 'w_att_proj': w_att_proj,
            'w_conv_proj': w_conv_proj, 'b_conv_proj': b_conv_proj, 'w_out': w_out,
            'ln_g': ln_g, 'ln_b': ln_b}


def reference(x, w_in, b_in, conv_w, conv_b, conv_ln_g, conv_ln_b,
              w_att_proj, w_conv_proj, b_conv_proj, w_out, ln_g, ln_b):
    for l in range(DEPTH):
        x = hybrid_layer(x, w_in[l], b_in[l], conv_w[l], conv_b[l], conv_ln_g[l], conv_ln_b[l],
                         w_att_proj[l], w_conv_proj[l], b_conv_proj[l], w_out[l], ln_g[l], ln_b[l])
    return x
```

```python
import functools
import math

import jax
import jax.numpy as jnp
from jax import lax
from jax.experimental import pallas as pl
from jax.experimental.pallas import tpu as pltpu

F32 = jnp.float32
BF16 = jnp.bfloat16

HEAD_DIM = 64
CONV_WIDTH = 31
LN_EPS = 1e-5
LOG2E = math.log2(math.e)

TQ = 256
TK = 256
HALO = 32
TM_QKV = 512
TM_MIX = 256
VMEM_LIMIT_BYTES = 56 * 1024 * 1024

_NT = (((1,), (1,)), ((), ()))


def _dot(a, b):
    return jnp.dot(a, b, preferred_element_type=F32)


def _dot_nt(a, b):
    return lax.dot_general(a, b, _NT, preferred_element_type=F32)


def _sigmoid(v):
    return 1.0 / (1.0 + jnp.exp(-v))


def _silu(v):
    return v * _sigmoid(v)


def _layer_norm(v, g, b):
    mu = jnp.mean(v, axis=-1, keepdims=True)
    d = v - mu
    var = jnp.mean(d * d, axis=-1, keepdims=True)
    return d * lax.rsqrt(var + LN_EPS) * g + b


def _qkv_kernel(x_ref, wk_ref, bk_ref, wqvT_ref, bqvT_ref, k_ref, qT_ref, vT_ref, *, n_heads, q_scale):
    xb = x_ref[0].astype(BF16)
    k = _dot(xb, wk_ref[0]) + bk_ref[0]
    k_ref[0] = k.astype(BF16)
    qvT = _dot_nt(wqvT_ref[0], xb) + bqvT_ref[0]
    d_att = n_heads * HEAD_DIM
    tm = xb.shape[0]
    zeros = jnp.zeros((HEAD_DIM, TQ), BF16)
    for h in range(n_heads):
        rows = slice(h * HEAD_DIM, (h + 1) * HEAD_DIM)
        vrows = slice(d_att + h * HEAD_DIM, d_att + (h + 1) * HEAD_DIM)
        lo = (h % 2) * HEAD_DIM
        other = HEAD_DIM - lo
        for c in range(tm // TQ):
            cols = slice(c * TQ, (c + 1) * TQ)
            qT_ref[0, h, c, lo:lo + HEAD_DIM, :] = (qvT[rows, cols] * q_scale).astype(BF16)
            qT_ref[0, h, c, other:other + HEAD_DIM, :] = zeros
            vT_ref[0, h, c] = qvT[vrows, cols].astype(BF16)


def _qkv_call(x, wk, bk, wqvT, bqvT, layer, n_heads):
    B, S, D = x.shape
    d_att = n_heads * HEAD_DIM
    tm = TM_QKV
    nq = S // TQ
    lay = lambda b, i: (layer, 0, 0)
    return pl.pallas_call(
        functools.partial(_qkv_kernel, n_heads=n_heads, q_scale=HEAD_DIM ** -0.5 * LOG2E),
        grid=(B, S // tm),
        in_specs=[
            pl.BlockSpec((1, tm, D), lambda b, i: (b, i, 0)),
            pl.BlockSpec((1, D, d_att), lay),
            pl.BlockSpec((1, 1, d_att), lay),
            pl.BlockSpec((1, 2 * d_att, D), lay),
            pl.BlockSpec((1, 2 * d_att, 1), lay),
        ],
        out_specs=[
            pl.BlockSpec((1, tm, d_att), lambda b, i: (b, i, 0)),
            pl.BlockSpec((1, n_heads, tm // TQ, 2 * HEAD_DIM, TQ), lambda b, i: (b, 0, i, 0, 0)),
            pl.BlockSpec((1, n_heads, tm // TK, HEAD_DIM, TK), lambda b, i: (b, 0, i, 0, 0)),
        ],
        out_shape=[
            jax.ShapeDtypeStruct((B, S, d_att), BF16),
            jax.ShapeDtypeStruct((B, n_heads, nq, 2 * HEAD_DIM, TQ), BF16),
            jax.ShapeDtypeStruct((B, n_heads, S // TK, HEAD_DIM, TK), BF16),
        ],
        compiler_params=pltpu.CompilerParams(
            dimension_semantics=("parallel", "parallel"), vmem_limit_bytes=VMEM_LIMIT_BYTES),
        name="qkv_proj",
    )(x, wk, bk, wqvT, bqvT)


def _attn_kernel(qT_ref, k_ref, vT_ref, ntri_ref, o_ref):
    qi = pl.program_id(2)
    qT = qT_ref[0, 0, 0]
    ntri = ntri_ref[...]

    def block(kb, carry, acc, mask):
        kblk = k_ref[0, pl.ds(pl.multiple_of(kb * TK, TK), TK), :]
        z = _dot(kblk, qT)
        e = jnp.exp2(-jnp.abs(z))
        p = jnp.maximum(z, 0.0) + jnp.log(1.0 + e) * LOG2E
        if mask is not None:
            p = jnp.where(mask, p, 0.0)
        g = _dot(ntri, p.astype(BF16)) + carry
        w = jnp.exp2(z + g)
        if mask is not None:
            w = jnp.where(mask, w, 0.0)
        acc = acc + _dot(vT_ref[0, 0, kb], w.astype(BF16))
        return g[0:1, :], acc

    key_pos = lax.broadcasted_iota(jnp.int32, (TK, TQ), 0)
    qry_pos = lax.broadcasted_iota(jnp.int32, (TK, TQ), 1)
    carry, acc = block(qi, jnp.zeros((1, TQ), F32), jnp.zeros((HEAD_DIM, TQ), F32), key_pos < qry_pos)

    def body(i, c):
        return block(qi - 1 - i, c[0], c[1], None)

    carry, acc = lax.fori_loop(0, qi, body, (carry, acc))
    o_ref[0] = acc


def _attn_call(qT, k, vT, ntri):
    B, H, nq = qT.shape[:3]
    S = k.shape[1]
    return pl.pallas_call(
        _attn_kernel,
        grid=(B, H, nq),
        in_specs=[
            pl.BlockSpec((1, 1, 1, 2 * HEAD_DIM, TQ), lambda b, h, q: (b, h, q, 0, 0)),
            pl.BlockSpec((1, S, 2 * HEAD_DIM), lambda b, h, q: (b, 0, h // 2)),
            pl.BlockSpec((1, 1, S // TK, HEAD_DIM, TK), lambda b, h, q: (b, h, 0, 0, 0)),
            pl.BlockSpec((TK, TK), lambda b, h, q: (0, 0)),
        ],
        out_specs=pl.BlockSpec((1, HEAD_DIM, TQ), lambda b, h, q: (b, h, q)),
        out_shape=jax.ShapeDtypeStruct((B, H * HEAD_DIM, S), F32),
        compiler_params=pltpu.CompilerParams(
            dimension_semantics=("parallel", "parallel", "parallel"), vmem_limit_bytes=VMEM_LIMIT_BYTES),
        name="stickbreak_attn",
    )(qT, k, vT, ntri)


CONV_ROWS = 64
LANES = 128


def _mix_kernel(x_ref, xh_ref, attT_ref, wzT_ref, bzT_ref, wr_ref, br_ref, cw_ref, cb_ref, clg_ref, clb_ref,
                wap_ref, wcp_ref, bcp_ref, wo_ref, lg_ref, lb_ref, o_ref, cext_ref, conv_ref, *, d_conv, alpha):
    i = pl.program_id(1)
    x = x_ref[0]
    xb = x.astype(BF16)
    tm, d_model = x.shape
    o_ga, o_gb, o_zc, o_gat, o_gcv, o_end = (0, d_conv, 2 * d_conv, 3 * d_conv, 3 * d_conv + d_model,
                                            3 * d_conv + 2 * d_model)

    def proj(lhs, lo, hi):
        return _dot(lhs, wr_ref[0, :, lo:hi]) + br_ref[0, :, lo:hi]

    c = proj(xb, o_ga, o_gb) * _sigmoid(proj(xb, o_gb, o_zc))
    xhb = xh_ref[0].astype(BF16)
    ch = proj(xhb, o_ga, o_gb) * _sigmoid(proj(xhb, o_gb, o_zc))
    cext_ref[0:HALO, :] = jnp.where(i > 0, ch, 0.0)
    cext_ref[HALO:, :] = c

    shift = HALO - (CONV_WIDTH - 1)
    for lb in range(d_conv // LANES):
        lanes = slice(lb * LANES, (lb + 1) * LANES)
        for rc in range(tm // CONV_ROWS):
            acc = jnp.broadcast_to(cb_ref[0, :, lanes], (CONV_ROWS, LANES))
            for j in range(CONV_WIDTH):
                r0 = rc * CONV_ROWS + j + shift
                acc = acc + cw_ref[0, j:j + 1, lanes] * cext_ref[r0:r0 + CONV_ROWS, lanes]
            conv_ref[rc * CONV_ROWS:(rc + 1) * CONV_ROWS, lanes] = acc

    cn = _silu(_layer_norm(conv_ref[...], clg_ref[0], clb_ref[0]))
    cg = cn * _silu(proj(xb, o_zc, o_gat))
    conv_branch = _dot(cg.astype(BF16), wcp_ref[0]) + bcp_ref[0]

    zaT = _dot_nt(wzT_ref[0], xb) + bzT_ref[0]
    gT = attT_ref[0] * _silu(zaT)
    att_branch = _dot(gT.T.astype(BF16), wap_ref[0])

    merged = (_sigmoid(proj(xb, o_gat, o_gcv)) * att_branch
              + _sigmoid(proj(xb, o_gcv, o_end)) * conv_branch)
    y = _dot(merged.astype(BF16), wo_ref[0])
    o_ref[0] = _layer_norm(alpha * x + y, lg_ref[0], lb_ref[0])


def _mix_call(x, attT, wzT, bzT, wr, br, cw, cb, clg, clb, wap, wcp, bcp, wo, lg, lb, layer, alpha):
    B, S, D = x.shape
    d_att = attT.shape[1]
    d_conv = cw.shape[-1]
    tm = TM_MIX
    lay = lambda b, i: (layer, 0, 0)
    full = lambda a: pl.BlockSpec((1,) + a.shape[1:], lay)
    return pl.pallas_call(
        functools.partial(_mix_kernel, d_conv=d_conv, alpha=alpha),
        grid=(B, S // tm),
        in_specs=[
            pl.BlockSpec((1, tm, D), lambda b, i: (b, i, 0)),
            pl.BlockSpec((1, HALO, D), lambda b, i: (b, jnp.maximum(i * (tm // HALO) - 1, 0), 0)),
            pl.BlockSpec((1, d_att, tm), lambda b, i: (b, 0, i)),
            full(wzT), full(bzT), full(wr), full(br), full(cw), full(cb), full(clg), full(clb),
            full(wap), full(wcp), full(bcp), full(wo), full(lg), full(lb),
        ],
        out_specs=pl.BlockSpec((1, tm, D), lambda b, i: (b, i, 0)),
        out_shape=jax.ShapeDtypeStruct((B, S, D), F32),
        scratch_shapes=[pltpu.VMEM((HALO + tm, d_conv), F32), pltpu.VMEM((tm, d_conv), F32)],
        compiler_params=pltpu.CompilerParams(
            dimension_semantics=("parallel", "parallel"), vmem_limit_bytes=VMEM_LIMIT_BYTES),
        name="mix_out_norm",
    )(x, x, attT, wzT, bzT, wr, br, cw, cb, clg, clb, wap, wcp, bcp, wo, lg, lb)


def kernel(x, w_in, b_in, conv_w, conv_b, conv_ln_g, conv_ln_b, w_att_proj, w_conv_proj, b_conv_proj, w_out,
           ln_g, ln_b):
    B, S, D = x.shape
    depth = w_in.shape[0]
    d_att = w_att_proj.shape[1]
    d_conv = conv_w.shape[-1]
    n_heads = d_att // HEAD_DIM
    assert S % TM_QKV == 0 and S % TM_MIX == 0 and TM_QKV % TQ == 0 and TQ == TK
    assert n_heads % 2 == 0 and d_conv % LANES == 0 and TM_MIX % CONV_ROWS == 0
    assert w_in.shape[-1] == 4 * d_att + 3 * d_conv + 2 * D
    alpha = (2 * depth) ** 0.25

    o_k, o_v, o_z, o_r = d_att, 2 * d_att, 3 * d_att, 4 * d_att
    wk = w_in[:, :, o_k:o_v].astype(BF16)
    bk = b_in[:, None, o_k:o_v]
    wqvT = jnp.swapaxes(jnp.concatenate([w_in[:, :, :o_k], w_in[:, :, o_v:o_z]], axis=-1), 1, 2).astype(BF16)
    bqvT = jnp.concatenate([b_in[:, :o_k], b_in[:, o_v:o_z]], axis=-1)[:, :, None]
    wzT = jnp.swapaxes(w_in[:, :, o_z:o_r], 1, 2).astype(BF16)
    bzT = b_in[:, o_z:o_r, None]
    wr = w_in[:, :, o_r:].astype(BF16)
    br = b_in[:, None, o_r:]
    row = lambda a: a[:, None, :]
    wap, wcp, wo = w_att_proj.astype(BF16), w_conv_proj.astype(BF16), w_out.astype(BF16)

    r = lax.broadcasted_iota(jnp.int32, (TK, TK), 0)
    c = lax.broadcasted_iota(jnp.int32, (TK, TK), 1)
    ntri = jnp.where(c >= r, -1.0, 0.0).astype(BF16)

    for l in range(depth):
        k, qT, vT = _qkv_call(x, wk, bk, wqvT, bqvT, l, n_heads)
        attT = _attn_call(qT, k, vT, ntri)
        x = _mix_call(x, attT, wzT, bzT, wr, br, conv_w, row(conv_b), row(conv_ln_g), row(conv_ln_b),
                      wap, wcp, row(b_conv_proj), wo, row(ln_g), row(ln_b), l, alpha)
    return x
```

```python
import functools
import math

import jax
import jax.numpy as jnp
from jax import lax
from jax.experimental import pallas as pl
from jax.experimental.pallas import tpu as pltpu

F32 = jnp.float32
BF16 = jnp.bfloat16

HEAD_DIM = 64
CONV_WIDTH = 31
LN_EPS = 1e-5
LOG2E = math.log2(math.e)

TQ = 256
TK = 256
SKEW_SUMS = 2
SKEW_VALUES = 2
HALO = 32
TM_QKV = 512
TM_MIX = 256
VMEM_LIMIT_BYTES = 56 * 1024 * 1024

_NT = (((1,), (1,)), ((), ()))


def _dot(a, b):
    return jnp.dot(a, b, preferred_element_type=F32)


def _dot_nt(a, b):
    return lax.dot_general(a, b, _NT, preferred_element_type=F32)


def _sigmoid(v):
    return 1.0 / (1.0 + jnp.exp(-v))


def _silu(v):
    return v * _sigmoid(v)


def _layer_norm(v, g, b):
    mu = jnp.mean(v, axis=-1, keepdims=True)
    d = v - mu
    var = jnp.mean(d * d, axis=-1, keepdims=True)
    return d * lax.rsqrt(var + LN_EPS) * g + b


def _qkv_kernel(x_ref, wk_ref, bk_ref, wqvT_ref, bqvT_ref, k_ref, qT_ref, vT_ref, *, n_heads, q_scale):
    xb = x_ref[0].astype(BF16)
    k = _dot(xb, wk_ref[0]) + bk_ref[0]
    k_ref[0] = k.astype(BF16)
    qvT = _dot_nt(wqvT_ref[0], xb) + bqvT_ref[0]
    d_att = n_heads * HEAD_DIM
    tm = xb.shape[0]
    zeros = jnp.zeros((HEAD_DIM, TQ), BF16)
    for h in range(n_heads):
        rows = slice(h * HEAD_DIM, (h + 1) * HEAD_DIM)
        vrows = slice(d_att + h * HEAD_DIM, d_att + (h + 1) * HEAD_DIM)
        lo = (h % 2) * HEAD_DIM
        other = HEAD_DIM - lo
        for c in range(tm // TQ):
            cols = slice(c * TQ, (c + 1) * TQ)
            qT_ref[0, h, c, lo:lo + HEAD_DIM, :] = (qvT[rows, cols] * q_scale).astype(BF16)
            qT_ref[0, h, c, other:other + HEAD_DIM, :] = zeros
            vT_ref[0, h, c] = qvT[vrows, cols].astype(BF16)


def _qkv_call(x, wk, bk, wqvT, bqvT, layer, n_heads):
    B, S, D = x.shape
    d_att = n_heads * HEAD_DIM
    tm = TM_QKV
    nq = S // TQ
    lay = lambda b, i: (layer, 0, 0)
    return pl.pallas_call(
        functools.partial(_qkv_kernel, n_heads=n_heads, q_scale=HEAD_DIM ** -0.5 * LOG2E),
        grid=(B, S // tm),
        in_specs=[
            pl.BlockSpec((1, tm, D), lambda b, i: (b, i, 0)),
            pl.BlockSpec((1, D, d_att), lay),
            pl.BlockSpec((1, 1, d_att), lay),
            pl.BlockSpec((1, 2 * d_att, D), lay),
            pl.BlockSpec((1, 2 * d_att, 1), lay),
        ],
        out_specs=[
            pl.BlockSpec((1, tm, d_att), lambda b, i: (b, i, 0)),
            pl.BlockSpec((1, n_heads, tm // TQ, 2 * HEAD_DIM, TQ), lambda b, i: (b, 0, i, 0, 0)),
            pl.BlockSpec((1, n_heads, tm // TK, HEAD_DIM, TK), lambda b, i: (b, 0, i, 0, 0)),
        ],
        out_shape=[
            jax.ShapeDtypeStruct((B, S, d_att), BF16),
            jax.ShapeDtypeStruct((B, n_heads, nq, 2 * HEAD_DIM, TQ), BF16),
            jax.ShapeDtypeStruct((B, n_heads, S // TK, HEAD_DIM, TK), BF16),
        ],
        compiler_params=pltpu.CompilerParams(
            dimension_semantics=("parallel", "parallel"), vmem_limit_bytes=VMEM_LIMIT_BYTES),
        name="qkv_proj",
    )(x, wk, bk, wqvT, bqvT)


def _attn_kernel(qT_ref, k_ref, vT_ref, ntri_ref, o_ref, zbuf_ref, gbuf_ref, *, n_heads):
    qi = pl.program_id(1)
    ntri = ntri_ref[...]
    sign = jnp.uint32(0x80000000)

    def scores(h, kb):
        pair = slice((h // 2) * 2 * HEAD_DIM, (h // 2 + 1) * 2 * HEAD_DIM)
        kblk = k_ref[0, pl.ds(pl.multiple_of(kb * TK, TK), TK), pair]
        return _dot(kblk, qT_ref[0, h, 0])

    def log_fail_sums(z, mask):
        neg_abs = lax.bitcast_convert_type(lax.bitcast_convert_type(z, jnp.uint32) | sign, F32)
        p = jnp.maximum(z, 0.0) + jnp.log(1.0 + jnp.exp2(neg_abs)) * LOG2E
        if mask is not None:
            p = jnp.where(mask, p, 0.0)
        return _dot(ntri, p.astype(BF16))

    def weighted_values(h, kb, z, g, offset, mask):
        w = jnp.exp2(z + g)
        if mask is not None:
            w = jnp.where(mask, w, 0.0)
        return _dot(vT_ref[0, h, kb], w.astype(BF16)) * jnp.exp2(offset)

    def rows(h):
        return slice(h * HEAD_DIM, (h + 1) * HEAD_DIM)

    d1, d2 = SKEW_SUMS, SKEW_VALUES
    n_park = d1 + d2

    def stages(step, kb, carries, mask, cur, prev):
        z, g = cur
        offs, used = carries
        h = step - d1
        if h >= 0:
            if h in z:
                g[h] = log_fail_sums(z[h], mask)
                used[h], offs[h] = offs[h], offs[h] + g[h][0:1, :]
        elif prev is not None:
            hp = n_heads + h
            g[h] = log_fail_sums(zbuf_ref[hp - (n_heads - n_park)], prev[1])
            used[hp], offs[hp] = offs[hp], offs[hp] + g[h][0:1, :]
        h = step - d1 - d2
        if h >= 0:
            pv = weighted_values(h, kb, z.pop(h), g.pop(h), used[h], mask)
            if prev is None:
                o_ref[0, rows(h), :] = pv
            else:
                o_ref[0, rows(h), :] += pv
        elif prev is not None:
            hp = n_heads + h
            slot = hp - (n_heads - n_park)
            gp = g.pop(h) if h in g else gbuf_ref[slot]
            o_ref[0, rows(hp), :] += weighted_values(hp, prev[0], zbuf_ref[slot], gp, used[hp], prev[1])

    def key_block(kb, carries, mask, prev):
        carries = (list(carries[0]), list(carries[1]))
        z, g = {}, {}
        for step in range(n_heads):
            z[step] = scores(step, kb)
            stages(step, kb, carries, mask, (z, g), prev)
        for slot in range(n_park):
            h = n_heads - n_park + slot
            zbuf_ref[slot] = z[h]
            if slot < d2:
                gbuf_ref[slot] = g[h]
        return tuple(carries[0]), tuple(carries[1])

    key_pos = lax.broadcasted_iota(jnp.int32, (TK, TQ), 0)
    qry_pos = lax.broadcasted_iota(jnp.int32, (TK, TQ), 1)
    diag_mask = key_pos < qry_pos
    zero = jnp.zeros((1, TQ), F32)
    for h in range(n_heads - n_park, n_heads):
        o_ref[0, rows(h), :] = jnp.zeros((HEAD_DIM, TQ), F32)
    carries = key_block(qi, ((zero,) * n_heads, (zero,) * n_heads), diag_mask, None)

    def body(i, carries):
        pmask = jnp.logical_or(diag_mask, i > 0)
        return key_block(qi - 1 - i, carries, None, (qi - i, pmask))

    carries = lax.fori_loop(0, qi, body, carries)
    carries = (list(carries[0]), list(carries[1]))
    pmask = jnp.logical_or(diag_mask, qi > 0)
    g = {}
    for step in range(n_park):
        stages(step, 0, carries, None, ({}, g), (0, pmask))


def _attn_call(qT, k, vT, ntri):
    B, H, nq = qT.shape[:3]
    S = k.shape[1]
    return pl.pallas_call(
        functools.partial(_attn_kernel, n_heads=H),
        grid=(B, nq),
        in_specs=[
            pl.BlockSpec((1, H, 1, 2 * HEAD_DIM, TQ), lambda b, q: (b, 0, q, 0, 0)),
            pl.BlockSpec((1, S, H * HEAD_DIM), lambda b, q: (b, 0, 0)),
            pl.BlockSpec((1, H, S // TK, HEAD_DIM, TK), lambda b, q: (b, 0, 0, 0, 0)),
            pl.BlockSpec((TK, TK), lambda b, q: (0, 0)),
        ],
        out_specs=pl.BlockSpec((1, H * HEAD_DIM, TQ), lambda b, q: (b, 0, q)),
        out_shape=jax.ShapeDtypeStruct((B, H * HEAD_DIM, S), F32),
        scratch_shapes=[pltpu.VMEM((SKEW_SUMS + SKEW_VALUES, TK, TQ), F32),
                        pltpu.VMEM((SKEW_VALUES, TK, TQ), F32)],
        compiler_params=pltpu.CompilerParams(
            dimension_semantics=("parallel", "parallel"), vmem_limit_bytes=VMEM_LIMIT_BYTES),
        name="stickbreak_attn",
    )(qT, k, vT, ntri)


CONV_ROWS = 64
LANES = 128
SUBLANES = 8


def _mix_kernel(x_ref, xh_ref, attT_ref, wzT_ref, bzT_ref, wr_ref, br_ref, cw_ref, cb_ref, clg_ref, clb_ref,
                wap_ref, wcp_ref, bcp_ref, wo_ref, lg_ref, lb_ref, o_ref, cext_ref, shifted_ref, conv_ref, *,
                d_conv, alpha):
    i = pl.program_id(1)
    x = x_ref[0]
    xb = x.astype(BF16)
    tm, d_model = x.shape
    o_ga, o_gb, o_zc, o_gat, o_gcv, o_end = (0, d_conv, 2 * d_conv, 3 * d_conv, 3 * d_conv + d_model,
                                            3 * d_conv + 2 * d_model)

    def proj(lhs, lo, hi):
        return _dot(lhs, wr_ref[0, :, lo:hi]) + br_ref[0, :, lo:hi]

    c = proj(xb, o_ga, o_gb) * _sigmoid(proj(xb, o_gb, o_zc))
    xhb = xh_ref[0].astype(BF16)
    ch = proj(xhb, o_ga, o_gb) * _sigmoid(proj(xhb, o_gb, o_zc))
    cext_ref[0:HALO, :] = jnp.where(i > 0, ch, 0.0)
    cext_ref[HALO:, :] = c

    def conv_lanes(lb):
        lanes = slice(lb * LANES, (lb + 1) * LANES)
        lead = HALO - (CONV_WIDTH - 1)
        span = tm + HALO - SUBLANES
        for s in range(1, SUBLANES):
            shifted_ref[s - 1, :, :] = cext_ref[s:s + span, lanes]
        for rc in range(tm // CONV_ROWS):
            acc = jnp.broadcast_to(cb_ref[0, :, lanes], (CONV_ROWS, LANES))
            for j in range(CONV_WIDTH):
                s, r0 = (j + lead) % SUBLANES, rc * CONV_ROWS + (j + lead) // SUBLANES * SUBLANES
                if s == 0:
                    rows = cext_ref[r0:r0 + CONV_ROWS, lanes]
                else:
                    rows = shifted_ref[s - 1, r0:r0 + CONV_ROWS, :]
                acc = acc + cw_ref[0, j:j + 1, lanes] * rows
            conv_ref[rc * CONV_ROWS:(rc + 1) * CONV_ROWS, lanes] = acc

    zaT = _dot_nt(wzT_ref[0], xb) + bzT_ref[0]
    conv_lanes(0)
    gT = attT_ref[0] * _silu(zaT)
    att_branch = _dot(gT.T.astype(BF16), wap_ref[0])
    gate_att = proj(xb, o_gat, o_gcv)
    conv_lanes(1)
    att_gated = _sigmoid(gate_att) * att_branch
    z_conv = proj(xb, o_zc, o_gat)
    conv_lanes(2)
    z_conv = _silu(z_conv)
    gate_conv = proj(xb, o_gcv, o_end)
    conv_lanes(3)
    gate_conv = _sigmoid(gate_conv)

    cg = _silu(_layer_norm(conv_ref[...], clg_ref[0], clb_ref[0])) * z_conv
    conv_branch = _dot(cg.astype(BF16), wcp_ref[0]) + bcp_ref[0]
    merged = att_gated + gate_conv * conv_branch
    y = _dot(merged.astype(BF16), wo_ref[0])
    o_ref[0] = _layer_norm(alpha * x + y, lg_ref[0], lb_ref[0])


def _mix_call(x, attT, wzT, bzT, wr, br, cw, cb, clg, clb, wap, wcp, bcp, wo, lg, lb, layer, alpha):
    B, S, D = x.shape
    d_att = attT.shape[1]
    d_conv = cw.shape[-1]
    tm = TM_MIX
    lay = lambda b, i: (layer, 0, 0)
    full = lambda a: pl.BlockSpec((1,) + a.shape[1:], lay)
    return pl.pallas_call(
        functools.partial(_mix_kernel, d_conv=d_conv, alpha=alpha),
        grid=(B, S // tm),
        in_specs=[
            pl.BlockSpec((1, tm, D), lambda b, i: (b, i, 0)),
            pl.BlockSpec((1, HALO, D), lambda b, i: (b, jnp.maximum(i * (tm // HALO) - 1, 0), 0)),
            pl.BlockSpec((1, d_att, tm), lambda b, i: (b, 0, i)),
            full(wzT), full(bzT), full(wr), full(br), full(cw), full(cb), full(clg), full(clb),
            full(wap), full(wcp), full(bcp), full(wo), full(lg), full(lb),
        ],
        out_specs=pl.BlockSpec((1, tm, D), lambda b, i: (b, i, 0)),
        out_shape=jax.ShapeDtypeStruct((B, S, D), F32),
        scratch_shapes=[pltpu.VMEM((HALO + tm, d_conv), F32),
                        pltpu.VMEM((SUBLANES - 1, HALO + tm - SUBLANES, LANES), F32),
                        pltpu.VMEM((tm, d_conv), F32)],
        compiler_params=pltpu.CompilerParams(
            dimension_semantics=("parallel", "parallel"), vmem_limit_bytes=VMEM_LIMIT_BYTES),
        name="mix_out_norm",
    )(x, x, attT, wzT, bzT, wr, br, cw, cb, clg, clb, wap, wcp, bcp, wo, lg, lb)


def kernel(x, w_in, b_in, conv_w, conv_b, conv_ln_g, conv_ln_b, w_att_proj, w_conv_proj, b_conv_proj, w_out,
           ln_g, ln_b):
    B, S, D = x.shape
    depth = w_in.shape[0]
    d_att = w_att_proj.shape[1]
    d_conv = conv_w.shape[-1]
    n_heads = d_att // HEAD_DIM
    assert S % TM_QKV == 0 and S % TM_MIX == 0 and TM_QKV % TQ == 0 and TQ == TK
    assert n_heads % 2 == 0 and d_conv == 4 * LANES and TM_MIX % CONV_ROWS == 0
    assert n_heads >= SKEW_SUMS + SKEW_VALUES and CONV_WIDTH - 1 <= HALO
    assert w_in.shape[-1] == 4 * d_att + 3 * d_conv + 2 * D
    alpha = (2 * depth) ** 0.25

    o_k, o_v, o_z, o_r = d_att, 2 * d_att, 3 * d_att, 4 * d_att
    wk = w_in[:, :, o_k:o_v].astype(BF16)
    bk = b_in[:, None, o_k:o_v]
    wqvT = jnp.swapaxes(jnp.concatenate([w_in[:, :, :o_k], w_in[:, :, o_v:o_z]], axis=-1), 1, 2).astype(BF16)
    bqvT = jnp.concatenate([b_in[:, :o_k], b_in[:, o_v:o_z]], axis=-1)[:, :, None]
    wzT = jnp.swapaxes(w_in[:, :, o_z:o_r], 1, 2).astype(BF16)
    bzT = b_in[:, o_z:o_r, None]
    wr = w_in[:, :, o_r:].astype(BF16)
    br = b_in[:, None, o_r:]
    row = lambda a: a[:, None, :]
    wap, wcp, wo = w_att_proj.astype(BF16), w_conv_proj.astype(BF16), w_out.astype(BF16)

    r = lax.broadcasted_iota(jnp.int32, (TK, TK), 0)
    c = lax.broadcasted_iota(jnp.int32, (TK, TK), 1)
    ntri = jnp.where(c >= r, -1.0, 0.0).astype(BF16)

    for l in range(depth):
        k, qT, vT = _qkv_call(x, wk, bk, wqvT, bqvT, l, n_heads)
        attT = _attn_call(qT, k, vT, ntri)
        x = _mix_call(x, attT, wzT, bzT, wr, br, conv_w, row(conv_b), row(conv_ln_g), row(conv_ln_b),
                      wap, wcp, row(b_conv_proj), wo, row(ln_g), row(ln_b), l, alpha)
    return x
```

```python
import functools
import math

import jax
import jax.numpy as jnp
from jax import lax
from jax.experimental import pallas as pl
from jax.experimental.pallas import tpu as pltpu

F32 = jnp.float32
BF16 = jnp.bfloat16

HEAD_DIM = 64
CONV_WIDTH = 31
LN_EPS = 1e-5
LOG2E = math.log2(math.e)

TQ = 256
TK = 256
SKEW_SUMS = 2
SKEW_VALUES = 2
HALO = 32
TM_QKV = 512
TM_MIX = 512
VMEM_LIMIT_BYTES = 56 * 1024 * 1024

_NT = (((1,), (1,)), ((), ()))


def _dot(a, b):
    return jnp.dot(a, b, preferred_element_type=F32)


def _dot_nt(a, b):
    return lax.dot_general(a, b, _NT, preferred_element_type=F32)


def _sigmoid(v):
    return 1.0 / (1.0 + jnp.exp(-v))


def _silu(v):
    return v * _sigmoid(v)


def _layer_norm(v, g, b):
    mu = jnp.mean(v, axis=-1, keepdims=True)
    d = v - mu
    var = jnp.mean(d * d, axis=-1, keepdims=True)
    return d * lax.rsqrt(var + LN_EPS) * g + b


def _qkv_kernel(x_ref, wk_ref, bk_ref, wqvT_ref, bqvT_ref, k_ref, qT_ref, vT_ref, *, n_heads, q_scale):
    xb = x_ref[0].astype(BF16)
    k = _dot(xb, wk_ref[0]) + bk_ref[0]
    k_ref[0] = k.astype(BF16)
    qvT = _dot_nt(wqvT_ref[0], xb) + bqvT_ref[0]
    d_att = n_heads * HEAD_DIM
    tm = xb.shape[0]
    zeros = jnp.zeros((HEAD_DIM, TQ), BF16)
    for h in range(n_heads):
        rows = slice(h * HEAD_DIM, (h + 1) * HEAD_DIM)
        vrows = slice(d_att + h * HEAD_DIM, d_att + (h + 1) * HEAD_DIM)
        lo = (h % 2) * HEAD_DIM
        other = HEAD_DIM - lo
        for c in range(tm // TQ):
            cols = slice(c * TQ, (c + 1) * TQ)
            qT_ref[0, h, c, lo:lo + HEAD_DIM, :] = (qvT[rows, cols] * q_scale).astype(BF16)
            qT_ref[0, h, c, other:other + HEAD_DIM, :] = zeros
            vT_ref[0, h, c] = qvT[vrows, cols].astype(BF16)


def _qkv_call(x, wk, bk, wqvT, bqvT, layer, n_heads):
    B, S, D = x.shape
    d_att = n_heads * HEAD_DIM
    tm = TM_QKV
    nq = S // TQ
    lay = lambda b, i: (layer, 0, 0)
    return pl.pallas_call(
        functools.partial(_qkv_kernel, n_heads=n_heads, q_scale=HEAD_DIM ** -0.5 * LOG2E),
        grid=(B, S // tm),
        in_specs=[
            pl.BlockSpec((1, tm, D), lambda b, i: (b, i, 0)),
            pl.BlockSpec((1, D, d_att), lay),
            pl.BlockSpec((1, 1, d_att), lay),
            pl.BlockSpec((1, 2 * d_att, D), lay),
            pl.BlockSpec((1, 2 * d_att, 1), lay),
        ],
        out_specs=[
            pl.BlockSpec((1, tm, d_att), lambda b, i: (b, i, 0)),
            pl.BlockSpec((1, n_heads, tm // TQ, 2 * HEAD_DIM, TQ), lambda b, i: (b, 0, i, 0, 0)),
            pl.BlockSpec((1, n_heads, tm // TK, HEAD_DIM, TK), lambda b, i: (b, 0, i, 0, 0)),
        ],
        out_shape=[
            jax.ShapeDtypeStruct((B, S, d_att), BF16),
            jax.ShapeDtypeStruct((B, n_heads, nq, 2 * HEAD_DIM, TQ), BF16),
            jax.ShapeDtypeStruct((B, n_heads, S // TK, HEAD_DIM, TK), BF16),
        ],
        compiler_params=pltpu.CompilerParams(
            dimension_semantics=("parallel", "parallel"), vmem_limit_bytes=VMEM_LIMIT_BYTES),
        name="qkv_proj",
    )(x, wk, bk, wqvT, bqvT)


def _attn_kernel(qT_ref, k_ref, vT_ref, ntri_ref, o_ref, zbuf_ref, gbuf_ref, *, n_heads):
    qi = pl.program_id(1)
    ntri = ntri_ref[...]
    sign = jnp.uint32(0x80000000)

    def scores(h, kb):
        pair = slice((h // 2) * 2 * HEAD_DIM, (h // 2 + 1) * 2 * HEAD_DIM)
        kblk = k_ref[0, pl.ds(pl.multiple_of(kb * TK, TK), TK), pair]
        return _dot(kblk, qT_ref[0, h, 0])

    def log_fail_sums(z, mask):
        neg_abs = lax.bitcast_convert_type(lax.bitcast_convert_type(z, jnp.uint32) | sign, F32)
        p = jnp.maximum(z, 0.0) + jnp.log(1.0 + jnp.exp2(neg_abs)) * LOG2E
        if mask is not None:
            p = jnp.where(mask, p, 0.0)
        return _dot(ntri, p.astype(BF16))

    def weighted_values(h, kb, z, g, offset, mask):
        w = jnp.exp2(z + g)
        if mask is not None:
            w = jnp.where(mask, w, 0.0)
        return _dot(vT_ref[0, h, kb], w.astype(BF16)) * jnp.exp2(offset)

    def rows(h):
        return slice(h * HEAD_DIM, (h + 1) * HEAD_DIM)

    d1, d2 = SKEW_SUMS, SKEW_VALUES
    n_park = d1 + d2

    def stages(step, kb, carries, mask, cur, prev):
        z, g = cur
        offs, used = carries
        h = step - d1
        if h >= 0:
            if h in z:
                g[h] = log_fail_sums(z[h], mask)
                used[h], offs[h] = offs[h], offs[h] + g[h][0:1, :]
        elif prev is not None:
            hp = n_heads + h
            g[h] = log_fail_sums(zbuf_ref[hp - (n_heads - n_park)], prev[1])
            used[hp], offs[hp] = offs[hp], offs[hp] + g[h][0:1, :]
        h = step - d1 - d2
        if h >= 0:
            pv = weighted_values(h, kb, z.pop(h), g.pop(h), used[h], mask)
            if prev is None:
                o_ref[0, rows(h), :] = pv
            else:
                o_ref[0, rows(h), :] += pv
        elif prev is not None:
            hp = n_heads + h
            slot = hp - (n_heads - n_park)
            gp = g.pop(h) if h in g else gbuf_ref[slot]
            o_ref[0, rows(hp), :] += weighted_values(hp, prev[0], zbuf_ref[slot], gp, used[hp], prev[1])

    def key_block(kb, carries, mask, prev):
        carries = (list(carries[0]), list(carries[1]))
        z, g = {}, {}
        for step in range(n_heads):
            z[step] = scores(step, kb)
            stages(step, kb, carries, mask, (z, g), prev)
        for slot in range(n_park):
            h = n_heads - n_park + slot
            zbuf_ref[slot] = z[h]
            if slot < d2:
                gbuf_ref[slot] = g[h]
        return tuple(carries[0]), tuple(carries[1])

    key_pos = lax.broadcasted_iota(jnp.int32, (TK, TQ), 0)
    qry_pos = lax.broadcasted_iota(jnp.int32, (TK, TQ), 1)
    diag_mask = key_pos < qry_pos
    zero = jnp.zeros((1, TQ), F32)
    for h in range(n_heads - n_park, n_heads):
        o_ref[0, rows(h), :] = jnp.zeros((HEAD_DIM, TQ), F32)
    carries = key_block(qi, ((zero,) * n_heads, (zero,) * n_heads), diag_mask, None)

    def any_weight_left(offs):
        top = functools.reduce(jnp.maximum, offs)
        return (jnp.max(jnp.exp2(top)) > 0.0).astype(jnp.int32)

    def more_blocks(state):
        i, alive, _, _ = state
        return jnp.logical_and(i < qi, alive > 0)

    def body(state):
        i, _, offs, used = state
        pmask = jnp.logical_or(diag_mask, i > 0)
        offs, used = key_block(qi - 1 - i, (offs, used), None, (qi - i, pmask))
        return i + 1, any_weight_left(offs), offs, used

    n_done, _, offs, used = lax.while_loop(
        more_blocks, body, (jnp.int32(0), any_weight_left(carries[0])) + carries)
    carries = (list(offs), list(used))
    pmask = jnp.logical_or(diag_mask, n_done > 0)
    g = {}
    for step in range(n_park):
        stages(step, qi - n_done, carries, None, ({}, g), (qi - n_done, pmask))


def _attn_call(qT, k, vT, ntri):
    B, H, nq = qT.shape[:3]
    S = k.shape[1]
    return pl.pallas_call(
        functools.partial(_attn_kernel, n_heads=H),
        grid=(B, nq),
        in_specs=[
            pl.BlockSpec((1, H, 1, 2 * HEAD_DIM, TQ), lambda b, q: (b, 0, q, 0, 0)),
            pl.BlockSpec((1, S, H * HEAD_DIM), lambda b, q: (b, 0, 0)),
            pl.BlockSpec((1, H, S // TK, HEAD_DIM, TK), lambda b, q: (b, 0, 0, 0, 0)),
            pl.BlockSpec((TK, TK), lambda b, q: (0, 0)),
        ],
        out_specs=pl.BlockSpec((1, H * HEAD_DIM, TQ), lambda b, q: (b, 0, q)),
        out_shape=jax.ShapeDtypeStruct((B, H * HEAD_DIM, S), F32),
        scratch_shapes=[pltpu.VMEM((SKEW_SUMS + SKEW_VALUES, TK, TQ), F32),
                        pltpu.VMEM((SKEW_VALUES, TK, TQ), F32)],
        compiler_params=pltpu.CompilerParams(
            dimension_semantics=("parallel", "parallel"), vmem_limit_bytes=VMEM_LIMIT_BYTES),
        name="stickbreak_attn",
    )(qT, k, vT, ntri)


CONV_ROWS = 64
LANES = 128
SUBLANES = 8


def _mix_kernel(x_ref, xh_ref, attT_ref, wzT_ref, bzT_ref, wr_ref, br_ref, cw_ref, cb_ref, clg_ref, clb_ref,
                wap_ref, wcp_ref, bcp_ref, wo_ref, lg_ref, lb_ref, o_ref, cext_ref, shifted_ref, conv_ref, *,
                d_conv, alpha):
    i = pl.program_id(1)
    x = x_ref[0]
    xb = x.astype(BF16)
    tm, d_model = x.shape
    o_ga, o_gb, o_zc, o_gat, o_gcv, o_end = (0, d_conv, 2 * d_conv, 3 * d_conv, 3 * d_conv + d_model,
                                            3 * d_conv + 2 * d_model)

    def proj(lhs, lo, hi):
        return _dot(lhs, wr_ref[0, :, lo:hi]) + br_ref[0, :, lo:hi]

    c = proj(xb, o_ga, o_gb) * _sigmoid(proj(xb, o_gb, o_zc))
    xhb = xh_ref[0].astype(BF16)
    ch = proj(xhb, o_ga, o_gb) * _sigmoid(proj(xhb, o_gb, o_zc))
    cext_ref[0:HALO, :] = jnp.where(i > 0, ch, 0.0)
    cext_ref[HALO:, :] = c

    def conv_lanes(lb):
        lanes = slice(lb * LANES, (lb + 1) * LANES)
        lead = HALO - (CONV_WIDTH - 1)
        span = tm + HALO - SUBLANES
        for s in range(1, SUBLANES):
            shifted_ref[s - 1, :, :] = cext_ref[s:s + span, lanes]
        for rc in range(tm // CONV_ROWS):
            acc = jnp.broadcast_to(cb_ref[0, :, lanes], (CONV_ROWS, LANES))
            for j in range(CONV_WIDTH):
                s, r0 = (j + lead) % SUBLANES, rc * CONV_ROWS + (j + lead) // SUBLANES * SUBLANES
                if s == 0:
                    rows = cext_ref[r0:r0 + CONV_ROWS, lanes]
                else:
                    rows = shifted_ref[s - 1, r0:r0 + CONV_ROWS, :]
                acc = acc + cw_ref[0, j:j + 1, lanes] * rows
            conv_ref[rc * CONV_ROWS:(rc + 1) * CONV_ROWS, lanes] = acc

    zaT = _dot_nt(wzT_ref[0], xb) + bzT_ref[0]
    conv_lanes(0)
    gT = attT_ref[0] * _silu(zaT)
    att_branch = _dot(gT.T.astype(BF16), wap_ref[0])
    gate_att = proj(xb, o_gat, o_gcv)
    conv_lanes(1)
    att_gated = _sigmoid(gate_att) * att_branch
    z_conv = proj(xb, o_zc, o_gat)
    conv_lanes(2)
    z_conv = _silu(z_conv)
    gate_conv = proj(xb, o_gcv, o_end)
    conv_lanes(3)
    gate_conv = _sigmoid(gate_conv)

    cg = _silu(_layer_norm(conv_ref[...], clg_ref[0], clb_ref[0])) * z_conv
    conv_branch = _dot(cg.astype(BF16), wcp_ref[0]) + bcp_ref[0]
    merged = att_gated + gate_conv * conv_branch
    y = _dot(merged.astype(BF16), wo_ref[0])
    o_ref[0] = _layer_norm(alpha * x + y, lg_ref[0], lb_ref[0])


def _mix_call(x, attT, wzT, bzT, wr, br, cw, cb, clg, clb, wap, wcp, bcp, wo, lg, lb, layer, alpha):
    B, S, D = x.shape
    d_att = attT.shape[1]
    d_conv = cw.shape[-1]
    tm = TM_MIX
    lay = lambda b, i: (layer, 0, 0)
    full = lambda a: pl.BlockSpec((1,) + a.shape[1:], lay, pipeline_mode=pl.Buffered(1))
    return pl.pallas_call(
        functools.partial(_mix_kernel, d_conv=d_conv, alpha=alpha),
        grid=(B, S // tm),
        in_specs=[
            pl.BlockSpec((1, tm, D), lambda b, i: (b, i, 0)),
            pl.BlockSpec((1, HALO, D), lambda b, i: (b, jnp.maximum(i * (tm // HALO) - 1, 0), 0)),
            pl.BlockSpec((1, d_att, tm), lambda b, i: (b, 0, i)),
            full(wzT), full(bzT), full(wr), full(br), full(cw), full(cb), full(clg), full(clb),
            full(wap), full(wcp), full(bcp), full(wo), full(lg), full(lb),
        ],
        out_specs=pl.BlockSpec((1, tm, D), lambda b, i: (b, i, 0)),
        out_shape=jax.ShapeDtypeStruct((B, S, D), F32),
        scratch_shapes=[pltpu.VMEM((HALO + tm, d_conv), F32),
                        pltpu.VMEM((SUBLANES - 1, HALO + tm - SUBLANES, LANES), F32),
                        pltpu.VMEM((tm, d_conv), F32)],
        compiler_params=pltpu.CompilerParams(
            dimension_semantics=("parallel", "parallel"), vmem_limit_bytes=VMEM_LIMIT_BYTES),
        name="mix_out_norm",
    )(x, x, attT, wzT, bzT, wr, br, cw, cb, clg, clb, wap, wcp, bcp, wo, lg, lb)


def kernel(x, w_in, b_in, conv_w, conv_b, conv_ln_g, conv_ln_b, w_att_proj, w_conv_proj, b_conv_proj, w_out,
           ln_g, ln_b):
    B, S, D = x.shape
    depth = w_in.shape[0]
    d_att = w_att_proj.shape[1]
    d_conv = conv_w.shape[-1]
    n_heads = d_att // HEAD_DIM
    assert S % TM_QKV == 0 and S % TM_MIX == 0 and TM_QKV % TQ == 0 and TQ == TK
    assert n_heads % 2 == 0 and d_conv == 4 * LANES and TM_MIX % CONV_ROWS == 0
    assert n_heads >= SKEW_SUMS + SKEW_VALUES and CONV_WIDTH - 1 <= HALO
    assert w_in.shape[-1] == 4 * d_att + 3 * d_conv + 2 * D
    alpha = (2 * depth) ** 0.25

    o_k, o_v, o_z, o_r = d_att, 2 * d_att, 3 * d_att, 4 * d_att
    wk = w_in[:, :, o_k:o_v].astype(BF16)
    bk = b_in[:, None, o_k:o_v]
    wqvT = jnp.swapaxes(jnp.concatenate([w_in[:, :, :o_k], w_in[:, :, o_v:o_z]], axis=-1), 1, 2).astype(BF16)
    bqvT = jnp.concatenate([b_in[:, :o_k], b_in[:, o_v:o_z]], axis=-1)[:, :, None]
    wzT = jnp.swapaxes(w_in[:, :, o_z:o_r], 1, 2).astype(BF16)
    bzT = b_in[:, o_z:o_r, None]
    wr = w_in[:, :, o_r:].astype(BF16)
    br = b_in[:, None, o_r:]
    row = lambda a: a[:, None, :]
    wap, wcp, wo = w_att_proj.astype(BF16), w_conv_proj.astype(BF16), w_out.astype(BF16)

    r = lax.broadcasted_iota(jnp.int32, (TK, TK), 0)
    c = lax.broadcasted_iota(jnp.int32, (TK, TK), 1)
    ntri = jnp.where(c >= r, -1.0, 0.0).astype(BF16)

    for l in range(depth):
        k, qT, vT = _qkv_call(x, wk, bk, wqvT, bqvT, l, n_heads)
        attT = _attn_call(qT, k, vT, ntri)
        x = _mix_call(x, attT, wzT, bzT, wr, br, conv_w, row(conv_b), row(conv_ln_g), row(conv_ln_b),
                      wap, wcp, row(b_conv_proj), wo, row(ln_g), row(ln_b), l, alpha)
    return x
```

```python
import functools
import math

import jax
import jax.numpy as jnp
from jax import lax
from jax.experimental import pallas as pl
from jax.experimental.pallas import tpu as pltpu

F32 = jnp.float32
BF16 = jnp.bfloat16

HEAD_DIM = 64
CONV_WIDTH = 31
LN_EPS = 1e-5
LOG2E = math.log2(math.e)

TQ = 256
TK = 256
SKEW_SUMS = 2
SKEW_VALUES = 2
TAIL_STEPS = 2
HALO = 32
TM_QKV = 512
TM_MIX = 512
VMEM_LIMIT_BYTES = 56 * 1024 * 1024

_NT = (((1,), (1,)), ((), ()))


def _dot(a, b):
    return jnp.dot(a, b, preferred_element_type=F32)


def _dot_nt(a, b):
    return lax.dot_general(a, b, _NT, preferred_element_type=F32)


def _sigmoid(v):
    return 1.0 / (1.0 + jnp.exp(-v))


def _silu(v):
    return v * _sigmoid(v)


def _layer_norm(v, g, b):
    mu = jnp.mean(v, axis=-1, keepdims=True)
    d = v - mu
    var = jnp.mean(d * d, axis=-1, keepdims=True)
    return d * lax.rsqrt(var + LN_EPS) * g + b


def _qkv_kernel(x_ref, wk_ref, bk_ref, wqvT_ref, bqvT_ref, k_ref, qT_ref, vT_ref, *, n_heads, q_scale):
    xb = x_ref[0].astype(BF16)
    k = _dot(xb, wk_ref[0]) + bk_ref[0]
    k_ref[0] = k.astype(BF16)
    qvT = _dot_nt(wqvT_ref[0], xb) + bqvT_ref[0]
    d_att = n_heads * HEAD_DIM
    tm = xb.shape[0]
    zeros = jnp.zeros((HEAD_DIM, TQ), BF16)
    for h in range(n_heads):
        rows = slice(h * HEAD_DIM, (h + 1) * HEAD_DIM)
        vrows = slice(d_att + h * HEAD_DIM, d_att + (h + 1) * HEAD_DIM)
        lo = (h % 2) * HEAD_DIM
        other = HEAD_DIM - lo
        for c in range(tm // TQ):
            cols = slice(c * TQ, (c + 1) * TQ)
            qT_ref[0, h, c, lo:lo + HEAD_DIM, :] = (qvT[rows, cols] * q_scale).astype(BF16)
            qT_ref[0, h, c, other:other + HEAD_DIM, :] = zeros
            vT_ref[0, h, c] = qvT[vrows, cols].astype(BF16)


def _qkv_call(x, wk, bk, wqvT, bqvT, layer, n_heads):
    B, S, D = x.shape
    d_att = n_heads * HEAD_DIM
    tm = TM_QKV
    nq = S // TQ
    lay = lambda b, i: (layer, 0, 0)
    return pl.pallas_call(
        functools.partial(_qkv_kernel, n_heads=n_heads, q_scale=HEAD_DIM ** -0.5 * LOG2E),
        grid=(B, S // tm),
        in_specs=[
            pl.BlockSpec((1, tm, D), lambda b, i: (b, i, 0)),
            pl.BlockSpec((1, D, d_att), lay),
            pl.BlockSpec((1, 1, d_att), lay),
            pl.BlockSpec((1, 2 * d_att, D), lay),
            pl.BlockSpec((1, 2 * d_att, 1), lay),
        ],
        out_specs=[
            pl.BlockSpec((1, tm, d_att), lambda b, i: (b, i, 0)),
            pl.BlockSpec((1, n_heads, tm // TQ, 2 * HEAD_DIM, TQ), lambda b, i: (b, 0, i, 0, 0)),
            pl.BlockSpec((1, n_heads, tm // TK, HEAD_DIM, TK), lambda b, i: (b, 0, i, 0, 0)),
        ],
        out_shape=[
            jax.ShapeDtypeStruct((B, S, d_att), BF16),
            jax.ShapeDtypeStruct((B, n_heads, nq, 2 * HEAD_DIM, TQ), BF16),
            jax.ShapeDtypeStruct((B, n_heads, S // TK, HEAD_DIM, TK), BF16),
        ],
        compiler_params=pltpu.CompilerParams(
            dimension_semantics=("parallel", "parallel"), vmem_limit_bytes=VMEM_LIMIT_BYTES),
        name="qkv_proj",
    )(x, wk, bk, wqvT, bqvT)


def _attn_kernel(qT_ref, k_ref, vT_ref, ntri_ref, o_ref, zbuf_ref, gbuf_ref, *, n_heads):
    qi = pl.program_id(1)
    ntri = ntri_ref[...]
    sign = jnp.uint32(0x80000000)

    def scores(h, kb):
        pair = slice((h // 2) * 2 * HEAD_DIM, (h // 2 + 1) * 2 * HEAD_DIM)
        kblk = k_ref[0, pl.ds(pl.multiple_of(kb * TK, TK), TK), pair]
        return _dot(kblk, qT_ref[0, h, 0])

    def log_fail_sums(z, mask):
        neg_abs = lax.bitcast_convert_type(lax.bitcast_convert_type(z, jnp.uint32) | sign, F32)
        p = jnp.maximum(z, 0.0) + jnp.log(1.0 + jnp.exp2(neg_abs)) * LOG2E
        if mask is not None:
            p = jnp.where(mask, p, 0.0)
        return _dot(ntri, p.astype(BF16))

    def weighted_values(h, kb, z, g, offset, mask):
        w = jnp.exp2(z + g)
        if mask is not None:
            w = jnp.where(mask, w, 0.0)
        return _dot(vT_ref[0, h, kb], w.astype(BF16)) * jnp.exp2(offset)

    def rows(h):
        return slice(h * HEAD_DIM, (h + 1) * HEAD_DIM)

    d1, d2, tail = SKEW_SUMS, SKEW_VALUES, TAIL_STEPS
    n_steps = n_heads + tail
    first_parked = n_steps - d1 - d2
    n_slots = n_heads - first_parked

    def stages(step, kb, carries, mask, cur, prev):
        z, g = cur
        offs, used = carries
        h = step - d1
        if h in z:
            g[h] = log_fail_sums(z[h], mask)
            used[h], offs[h] = offs[h], offs[h] + g[h][0:1, :]
        elif prev is not None and first_parked <= n_steps + h < n_heads:
            hp = n_steps + h
            g[h] = log_fail_sums(zbuf_ref[hp - first_parked], prev[1])
            used[hp], offs[hp] = offs[hp], offs[hp] + g[h][0:1, :]
        h = step - d1 - d2
        if h in z:
            pv = weighted_values(h, kb, z.pop(h), g.pop(h), used[h], mask)
            if prev is None:
                o_ref[0, rows(h), :] = pv
            else:
                o_ref[0, rows(h), :] += pv
        elif prev is not None and first_parked <= n_steps + h < n_heads:
            hp = n_steps + h
            slot = hp - first_parked
            gp = g.pop(h) if h in g else gbuf_ref[slot]
            o_ref[0, rows(hp), :] += weighted_values(hp, prev[0], zbuf_ref[slot], gp, used[hp], prev[1])

    def key_block(kb, carries, mask, prev):
        carries = (list(carries[0]), list(carries[1]))
        z, g = {}, {}
        for step in range(n_steps):
            if step < n_heads:
                z[step] = scores(step, kb)
            stages(step, kb, carries, mask, (z, g), prev)
        for slot in range(n_slots):
            h = first_parked + slot
            zbuf_ref[slot] = z[h]
            if h in g:
                gbuf_ref[slot] = g[h]
        return tuple(carries[0]), tuple(carries[1])

    key_pos = lax.broadcasted_iota(jnp.int32, (TK, TQ), 0)
    qry_pos = lax.broadcasted_iota(jnp.int32, (TK, TQ), 1)
    diag_mask = key_pos < qry_pos
    zero = jnp.zeros((1, TQ), F32)
    for h in range(first_parked, n_heads):
        o_ref[0, rows(h), :] = jnp.zeros((HEAD_DIM, TQ), F32)
    carries = key_block(qi, ((zero,) * n_heads, (zero,) * n_heads), diag_mask, None)

    def any_weight_left(offs):
        top = functools.reduce(jnp.maximum, offs)
        return (jnp.max(jnp.exp2(top)) > 0.0).astype(jnp.int32)

    def more_blocks(state):
        i, alive, _, _ = state
        return jnp.logical_and(i < qi, alive > 0)

    def body(state):
        i, _, offs, used = state
        pmask = jnp.logical_or(diag_mask, i > 0)
        offs, used = key_block(qi - 1 - i, (offs, used), None, (qi - i, pmask))
        return i + 1, any_weight_left(offs), offs, used

    n_done, _, offs, used = lax.while_loop(
        more_blocks, body, (jnp.int32(0), any_weight_left(carries[0])) + carries)
    carries = (list(offs), list(used))
    pmask = jnp.logical_or(diag_mask, n_done > 0)
    g = {}
    for step in range(n_slots):
        stages(step, qi - n_done, carries, None, ({}, g), (qi - n_done, pmask))


def _attn_call(qT, k, vT, ntri):
    B, H, nq = qT.shape[:3]
    S = k.shape[1]
    return pl.pallas_call(
        functools.partial(_attn_kernel, n_heads=H),
        grid=(B, nq),
        in_specs=[
            pl.BlockSpec((1, H, 1, 2 * HEAD_DIM, TQ), lambda b, q: (b, 0, q, 0, 0)),
            pl.BlockSpec((1, S, H * HEAD_DIM), lambda b, q: (b, 0, 0)),
            pl.BlockSpec((1, H, S // TK, HEAD_DIM, TK), lambda b, q: (b, 0, 0, 0, 0)),
            pl.BlockSpec((TK, TK), lambda b, q: (0, 0)),
        ],
        out_specs=pl.BlockSpec((1, H * HEAD_DIM, TQ), lambda b, q: (b, 0, q)),
        out_shape=jax.ShapeDtypeStruct((B, H * HEAD_DIM, S), F32),
        scratch_shapes=[pltpu.VMEM((SKEW_SUMS + SKEW_VALUES - TAIL_STEPS, TK, TQ), F32),
                        pltpu.VMEM((SKEW_VALUES, TK, TQ), F32)],
        compiler_params=pltpu.CompilerParams(
            dimension_semantics=("parallel", "parallel"), vmem_limit_bytes=VMEM_LIMIT_BYTES),
        name="stickbreak_attn",
    )(qT, k, vT, ntri)


CONV_ROWS = 64
LANES = 128
SUBLANES = 8


def _mix_kernel(x_ref, xh_ref, attT_ref, wzT_ref, bzT_ref, wr_ref, br_ref, cw_ref, cb_ref, clg_ref, clb_ref,
                wap_ref, wcp_ref, bcp_ref, wo_ref, lg_ref, lb_ref, o_ref, cext_ref, shifted_ref, conv_ref, *,
                d_conv, alpha):
    i = pl.program_id(1)
    x = x_ref[0]
    xb = x.astype(BF16)
    tm, d_model = x.shape
    o_ga, o_gb, o_zc, o_gat, o_gcv, o_end = (0, d_conv, 2 * d_conv, 3 * d_conv, 3 * d_conv + d_model,
                                            3 * d_conv + 2 * d_model)

    def proj(lhs, lo, hi):
        return _dot(lhs, wr_ref[0, :, lo:hi]) + br_ref[0, :, lo:hi]

    c = proj(xb, o_ga, o_gb) * _sigmoid(proj(xb, o_gb, o_zc))
    xhb = xh_ref[0].astype(BF16)
    ch = proj(xhb, o_ga, o_gb) * _sigmoid(proj(xhb, o_gb, o_zc))
    cext_ref[0:HALO, :] = jnp.where(i > 0, ch, 0.0)
    cext_ref[HALO:, :] = c

    def conv_lanes(lb):
        lanes = slice(lb * LANES, (lb + 1) * LANES)
        lead = HALO - (CONV_WIDTH - 1)
        span = tm + HALO - SUBLANES
        for s in range(1, SUBLANES):
            shifted_ref[s - 1, :, :] = cext_ref[s:s + span, lanes]
        for rc in range(tm // CONV_ROWS):
            acc = jnp.broadcast_to(cb_ref[0, :, lanes], (CONV_ROWS, LANES))
            for j in range(CONV_WIDTH):
                s, r0 = (j + lead) % SUBLANES, rc * CONV_ROWS + (j + lead) // SUBLANES * SUBLANES
                if s == 0:
                    rows = cext_ref[r0:r0 + CONV_ROWS, lanes]
                else:
                    rows = shifted_ref[s - 1, r0:r0 + CONV_ROWS, :]
                acc = acc + cw_ref[0, j:j + 1, lanes] * rows
            conv_ref[rc * CONV_ROWS:(rc + 1) * CONV_ROWS, lanes] = acc

    zaT = _dot_nt(wzT_ref[0], xb) + bzT_ref[0]
    conv_lanes(0)
    gT = attT_ref[0] * _silu(zaT)
    att_branch = _dot(gT.T.astype(BF16), wap_ref[0])
    gate_att = proj(xb, o_gat, o_gcv)
    conv_lanes(1)
    att_gated = _sigmoid(gate_att) * att_branch
    z_conv = proj(xb, o_zc, o_gat)
    conv_lanes(2)
    z_conv = _silu(z_conv)
    gate_conv = proj(xb, o_gcv, o_end)
    conv_lanes(3)
    gate_conv = _sigmoid(gate_conv)

    cg = _silu(_layer_norm(conv_ref[...], clg_ref[0], clb_ref[0])) * z_conv
    conv_branch = _dot(cg.astype(BF16), wcp_ref[0]) + bcp_ref[0]
    merged = att_gated + gate_conv * conv_branch
    y = _dot(merged.astype(BF16), wo_ref[0])
    o_ref[0] = _layer_norm(alpha * x + y, lg_ref[0], lb_ref[0])


def _mix_call(x, attT, wzT, bzT, wr, br, cw, cb, clg, clb, wap, wcp, bcp, wo, lg, lb, layer, alpha):
    B, S, D = x.shape
    d_att = attT.shape[1]
    d_conv = cw.shape[-1]
    tm = TM_MIX
    lay = lambda b, i: (layer, 0, 0)
    full = lambda a: pl.BlockSpec((1,) + a.shape[1:], lay, pipeline_mode=pl.Buffered(1))
    return pl.pallas_call(
        functools.partial(_mix_kernel, d_conv=d_conv, alpha=alpha),
        grid=(B, S // tm),
        in_specs=[
            pl.BlockSpec((1, tm, D), lambda b, i: (b, i, 0)),
            pl.BlockSpec((1, HALO, D), lambda b, i: (b, jnp.maximum(i * (tm // HALO) - 1, 0), 0)),
            pl.BlockSpec((1, d_att, tm), lambda b, i: (b, 0, i)),
            full(wzT), full(bzT), full(wr), full(br), full(cw), full(cb), full(clg), full(clb),
            full(wap), full(wcp), full(bcp), full(wo), full(lg), full(lb),
        ],
        out_specs=pl.BlockSpec((1, tm, D), lambda b, i: (b, i, 0)),
        out_shape=jax.ShapeDtypeStruct((B, S, D), F32),
        scratch_shapes=[pltpu.VMEM((HALO + tm, d_conv), F32),
                        pltpu.VMEM((SUBLANES - 1, HALO + tm - SUBLANES, LANES), F32),
                        pltpu.VMEM((tm, d_conv), F32)],
        compiler_params=pltpu.CompilerParams(
            dimension_semantics=("parallel", "parallel"), vmem_limit_bytes=VMEM_LIMIT_BYTES),
        name="mix_out_norm",
    )(x, x, attT, wzT, bzT, wr, br, cw, cb, clg, clb, wap, wcp, bcp, wo, lg, lb)


def kernel(x, w_in, b_in, conv_w, conv_b, conv_ln_g, conv_ln_b, w_att_proj, w_conv_proj, b_conv_proj, w_out,
           ln_g, ln_b):
    B, S, D = x.shape
    depth = w_in.shape[0]
    d_att = w_att_proj.shape[1]
    d_conv = conv_w.shape[-1]
    n_heads = d_att // HEAD_DIM
    assert S % TM_QKV == 0 and S % TM_MIX == 0 and TM_QKV % TQ == 0 and TQ == TK
    assert n_heads % 2 == 0 and d_conv == 4 * LANES and TM_MIX % CONV_ROWS == 0
    assert n_heads >= SKEW_SUMS + SKEW_VALUES and 0 <= TAIL_STEPS <= SKEW_SUMS and CONV_WIDTH - 1 <= HALO
    assert w_in.shape[-1] == 4 * d_att + 3 * d_conv + 2 * D
    alpha = (2 * depth) ** 0.25

    o_k, o_v, o_z, o_r = d_att, 2 * d_att, 3 * d_att, 4 * d_att
    wk = w_in[:, :, o_k:o_v].astype(BF16)
    bk = b_in[:, None, o_k:o_v]
    wqvT = jnp.swapaxes(jnp.concatenate([w_in[:, :, :o_k], w_in[:, :, o_v:o_z]], axis=-1), 1, 2).astype(BF16)
    bqvT = jnp.concatenate([b_in[:, :o_k], b_in[:, o_v:o_z]], axis=-1)[:, :, None]
    wzT = jnp.swapaxes(w_in[:, :, o_z:o_r], 1, 2).astype(BF16)
    bzT = b_in[:, o_z:o_r, None]
    wr = w_in[:, :, o_r:].astype(BF16)
    br = b_in[:, None, o_r:]
    row = lambda a: a[:, None, :]
    wap, wcp, wo = w_att_proj.astype(BF16), w_conv_proj.astype(BF16), w_out.astype(BF16)

    r = lax.broadcasted_iota(jnp.int32, (TK, TK), 0)
    c = lax.broadcasted_iota(jnp.int32, (TK, TK), 1)
    ntri = jnp.where(c >= r, -1.0, 0.0).astype(BF16)

    for l in range(depth):
        k, qT, vT = _qkv_call(x, wk, bk, wqvT, bqvT, l, n_heads)
        attT = _attn_call(qT, k, vT, ntri)
        x = _mix_call(x, attT, wzT, bzT, wr, br, conv_w, row(conv_b), row(conv_ln_g), row(conv_ln_b),
                      wap, wcp, row(b_conv_proj), wo, row(ln_g), row(ln_b), l, alpha)
    return x
```

```python
import functools
import math

import jax
import jax.numpy as jnp
from jax import lax
from jax.experimental import pallas as pl
from jax.experimental.pallas import tpu as pltpu

F32 = jnp.float32
BF16 = jnp.bfloat16

HEAD_DIM = 64
CONV_WIDTH = 31
LN_EPS = 1e-5
LOG2E = math.log2(math.e)

TQ = 256
TK = 256
SKEW_SUMS = 2
SKEW_VALUES = 2
TAIL_STEPS = 2
HALO = 32
TM_QKV = 512
TM_MIX = 512
VMEM_LIMIT_BYTES = 56 * 1024 * 1024

_NT = (((1,), (1,)), ((), ()))


def _dot(a, b):
    return jnp.dot(a, b, preferred_element_type=F32)


def _dot_nt(a, b):
    return lax.dot_general(a, b, _NT, preferred_element_type=F32)


def _sigmoid(v):
    return 0.5 * jnp.tanh(0.5 * v) + 0.5


def _silu(v):
    return v * _sigmoid(v)


def _layer_norm(v, g, b):
    mu = jnp.mean(v, axis=-1, keepdims=True)
    d = v - mu
    var = jnp.mean(d * d, axis=-1, keepdims=True)
    return d * lax.rsqrt(var + LN_EPS) * g + b


def _qkv_kernel(x_ref, wk_ref, bk_ref, wqvT_ref, bqvT_ref, k_ref, qT_ref, vT_ref, *, n_heads, q_scale):
    xb = x_ref[0].astype(BF16)
    k = _dot(xb, wk_ref[0]) + bk_ref[0]
    k_ref[0] = k.astype(BF16)
    qvT = _dot_nt(wqvT_ref[0], xb) + bqvT_ref[0]
    d_att = n_heads * HEAD_DIM
    tm = xb.shape[0]
    zeros = jnp.zeros((HEAD_DIM, TQ), BF16)
    for h in range(n_heads):
        rows = slice(h * HEAD_DIM, (h + 1) * HEAD_DIM)
        vrows = slice(d_att + h * HEAD_DIM, d_att + (h + 1) * HEAD_DIM)
        lo = (h % 2) * HEAD_DIM
        other = HEAD_DIM - lo
        for c in range(tm // TQ):
            cols = slice(c * TQ, (c + 1) * TQ)
            qT_ref[0, h, c, lo:lo + HEAD_DIM, :] = (qvT[rows, cols] * q_scale).astype(BF16)
            qT_ref[0, h, c, other:other + HEAD_DIM, :] = zeros
            vT_ref[0, h, c] = qvT[vrows, cols].astype(BF16)


def _qkv_call(x, wk, bk, wqvT, bqvT, layer, n_heads):
    B, S, D = x.shape
    d_att = n_heads * HEAD_DIM
    tm = TM_QKV
    nq = S // TQ
    lay = lambda b, i: (layer, 0, 0)
    return pl.pallas_call(
        functools.partial(_qkv_kernel, n_heads=n_heads, q_scale=HEAD_DIM ** -0.5 * LOG2E),
        grid=(B, S // tm),
        in_specs=[
            pl.BlockSpec((1, tm, D), lambda b, i: (b, i, 0)),
            pl.BlockSpec((1, D, d_att), lay),
            pl.BlockSpec((1, 1, d_att), lay),
            pl.BlockSpec((1, 2 * d_att, D), lay),
            pl.BlockSpec((1, 2 * d_att, 1), lay),
        ],
        out_specs=[
            pl.BlockSpec((1, tm, d_att), lambda b, i: (b, i, 0)),
            pl.BlockSpec((1, n_heads, tm // TQ, 2 * HEAD_DIM, TQ), lambda b, i: (b, 0, i, 0, 0)),
            pl.BlockSpec((1, n_heads, tm // TK, HEAD_DIM, TK), lambda b, i: (b, 0, i, 0, 0)),
        ],
        out_shape=[
            jax.ShapeDtypeStruct((B, S, d_att), BF16),
            jax.ShapeDtypeStruct((B, n_heads, nq, 2 * HEAD_DIM, TQ), BF16),
            jax.ShapeDtypeStruct((B, n_heads, S // TK, HEAD_DIM, TK), BF16),
        ],
        compiler_params=pltpu.CompilerParams(
            dimension_semantics=("parallel", "parallel"), vmem_limit_bytes=VMEM_LIMIT_BYTES),
        name="qkv_proj",
    )(x, wk, bk, wqvT, bqvT)


def _attn_kernel(qT_ref, k_ref, vT_ref, ntri_ref, o_ref, zbuf_ref, gbuf_ref, *, n_heads):
    qi = pl.program_id(1)
    ntri = ntri_ref[...]
    sign = jnp.uint32(0x80000000)

    def scores(h, kb):
        pair = slice((h // 2) * 2 * HEAD_DIM, (h // 2 + 1) * 2 * HEAD_DIM)
        kblk = k_ref[0, pl.ds(pl.multiple_of(kb * TK, TK), TK), pair]
        return _dot(kblk, qT_ref[0, h, 0])

    def log_fail_sums(z, mask):
        neg_abs = lax.bitcast_convert_type(lax.bitcast_convert_type(z, jnp.uint32) | sign, F32)
        p = jnp.maximum(z, 0.0) + jnp.log(1.0 + jnp.exp2(neg_abs)) * LOG2E
        if mask is not None:
            p = jnp.where(mask, p, 0.0)
        return _dot(ntri, p.astype(BF16))

    def weighted_values(h, kb, z, g, offset, mask):
        w = jnp.exp2(z + g)
        if mask is not None:
            w = jnp.where(mask, w, 0.0)
        return _dot(vT_ref[0, h, kb], w.astype(BF16)) * jnp.exp2(offset)

    def rows(h):
        return slice(h * HEAD_DIM, (h + 1) * HEAD_DIM)

    d1, d2, tail = SKEW_SUMS, SKEW_VALUES, TAIL_STEPS
    n_steps = n_heads + tail
    first_parked = n_steps - d1 - d2
    n_slots = n_heads - first_parked

    def stages(step, kb, carries, mask, cur, prev):
        z, g = cur
        offs, used = carries
        h = step - d1
        if h in z:
            g[h] = log_fail_sums(z[h], mask)
            used[h], offs[h] = offs[h], offs[h] + g[h][0:1, :]
        elif prev is not None and first_parked <= n_steps + h < n_heads:
            hp = n_steps + h
            g[h] = log_fail_sums(zbuf_ref[hp - first_parked], prev[1])
            used[hp], offs[hp] = offs[hp], offs[hp] + g[h][0:1, :]
        h = step - d1 - d2
        if h in z:
            pv = weighted_values(h, kb, z.pop(h), g.pop(h), used[h], mask)
            if prev is None:
                o_ref[0, rows(h), :] = pv
            else:
                o_ref[0, rows(h), :] += pv
        elif prev is not None and first_parked <= n_steps + h < n_heads:
            hp = n_steps + h
            slot = hp - first_parked
            gp = g.pop(h) if h in g else gbuf_ref[slot]
            o_ref[0, rows(hp), :] += weighted_values(hp, prev[0], zbuf_ref[slot], gp, used[hp], prev[1])

    def key_block(kb, carries, mask, prev):
        carries = (list(carries[0]), list(carries[1]))
        z, g = {}, {}
        for step in range(n_steps):
            if step < n_heads:
                z[step] = scores(step, kb)
            stages(step, kb, carries, mask, (z, g), prev)
        for slot in range(n_slots):
            h = first_parked + slot
            zbuf_ref[slot] = z[h]
            if h in g:
                gbuf_ref[slot] = g[h]
        return tuple(carries[0]), tuple(carries[1])

    key_pos = lax.broadcasted_iota(jnp.int32, (TK, TQ), 0)
    qry_pos = lax.broadcasted_iota(jnp.int32, (TK, TQ), 1)
    diag_mask = key_pos < qry_pos
    zero = jnp.zeros((1, TQ), F32)
    for h in range(first_parked, n_heads):
        o_ref[0, rows(h), :] = jnp.zeros((HEAD_DIM, TQ), F32)
    carries = key_block(qi, ((zero,) * n_heads, (zero,) * n_heads), diag_mask, None)

    def any_weight_left(offs):
        top = functools.reduce(jnp.maximum, offs)
        return (jnp.max(jnp.exp2(top)) > 0.0).astype(jnp.int32)

    def more_blocks(state):
        i, alive, _, _ = state
        return jnp.logical_and(i < qi, alive > 0)

    def body(state):
        i, _, offs, used = state
        pmask = jnp.logical_or(diag_mask, i > 0)
        offs, used = key_block(qi - 1 - i, (offs, used), None, (qi - i, pmask))
        return i + 1, any_weight_left(offs), offs, used

    n_done, _, offs, used = lax.while_loop(
        more_blocks, body, (jnp.int32(0), any_weight_left(carries[0])) + carries)
    carries = (list(offs), list(used))
    pmask = jnp.logical_or(diag_mask, n_done > 0)
    g = {}
    for step in range(n_slots):
        stages(step, qi - n_done, carries, None, ({}, g), (qi - n_done, pmask))


def _attn_call(qT, k, vT, ntri):
    B, H, nq = qT.shape[:3]
    S = k.shape[1]
    return pl.pallas_call(
        functools.partial(_attn_kernel, n_heads=H),
        grid=(B, nq),
        in_specs=[
            pl.BlockSpec((1, H, 1, 2 * HEAD_DIM, TQ), lambda b, q: (b, 0, q, 0, 0)),
            pl.BlockSpec((1, S, H * HEAD_DIM), lambda b, q: (b, 0, 0)),
            pl.BlockSpec((1, H, S // TK, HEAD_DIM, TK), lambda b, q: (b, 0, 0, 0, 0)),
            pl.BlockSpec((TK, TK), lambda b, q: (0, 0)),
        ],
        out_specs=pl.BlockSpec((1, H * HEAD_DIM, TQ), lambda b, q: (b, 0, q)),
        out_shape=jax.ShapeDtypeStruct((B, H * HEAD_DIM, S), F32),
        scratch_shapes=[pltpu.VMEM((SKEW_SUMS + SKEW_VALUES - TAIL_STEPS, TK, TQ), F32),
                        pltpu.VMEM((SKEW_VALUES, TK, TQ), F32)],
        compiler_params=pltpu.CompilerParams(
            dimension_semantics=("parallel", "parallel"), vmem_limit_bytes=VMEM_LIMIT_BYTES),
        name="stickbreak_attn",
    )(qT, k, vT, ntri)


CONV_ROWS = 64
LANES = 128
SUBLANES = 8


def _mix_kernel(x_ref, xh_ref, attT_ref, wzT_ref, bzT_ref, wr_ref, br_ref, cw_ref, cb_ref, clg_ref, clb_ref,
                wap_ref, wcp_ref, bcp_ref, wo_ref, lg_ref, lb_ref, o_ref, cext_ref, shifted_ref, conv_ref, *,
                d_conv, alpha):
    i = pl.program_id(1)
    x = x_ref[0]
    xb = x.astype(BF16)
    tm, d_model = x.shape
    o_ga, o_gb, o_zc, o_gat, o_gcv, o_end = (0, d_conv, 2 * d_conv, 3 * d_conv, 3 * d_conv + d_model,
                                            3 * d_conv + 2 * d_model)

    def proj(lhs, lo, hi):
        return _dot(lhs, wr_ref[0, :, lo:hi]) + br_ref[0, :, lo:hi]

    c = proj(xb, o_ga, o_gb) * _sigmoid(proj(xb, o_gb, o_zc))
    xhb = xh_ref[0].astype(BF16)
    ch = proj(xhb, o_ga, o_gb) * _sigmoid(proj(xhb, o_gb, o_zc))
    cext_ref[0:HALO, :] = jnp.where(i > 0, ch, 0.0)
    cext_ref[HALO:, :] = c

    gate_att = proj(xb, o_gat, o_gcv)
    z_conv = proj(xb, o_zc, o_gat)
    gate_conv = proj(xb, o_gcv, o_end)
    zaT = _dot_nt(wzT_ref[0], xb) + bzT_ref[0]

    lead = HALO - (CONV_WIDTH - 1)
    span = tm + HALO - SUBLANES
    for lb in range(d_conv // LANES):
        lanes = slice(lb * LANES, (lb + 1) * LANES)
        for r in range(1, SUBLANES):
            shifted_ref[r - 1, :, :] = cext_ref[r:r + span, lanes]
        for rc in range(tm // CONV_ROWS):
            acc = jnp.broadcast_to(cb_ref[0, :, lanes], (CONV_ROWS, LANES))
            for j in range(CONV_WIDTH):
                r, r0 = (j + lead) % SUBLANES, rc * CONV_ROWS + (j + lead) // SUBLANES * SUBLANES
                if r == 0:
                    taps = cext_ref[r0:r0 + CONV_ROWS, lanes]
                else:
                    taps = shifted_ref[r - 1, r0:r0 + CONV_ROWS, :]
                acc = acc + cw_ref[0, j:j + 1, lanes] * taps
            conv_ref[rc * CONV_ROWS:(rc + 1) * CONV_ROWS, lanes] = acc

    gT = attT_ref[0] * _silu(zaT)
    att_branch = _dot(gT.T.astype(BF16), wap_ref[0])
    cg = _silu(_layer_norm(conv_ref[...], clg_ref[0], clb_ref[0])) * _silu(z_conv)
    conv_branch = _dot(cg.astype(BF16), wcp_ref[0]) + bcp_ref[0]
    merged = _sigmoid(gate_att) * att_branch + _sigmoid(gate_conv) * conv_branch
    y = _dot(merged.astype(BF16), wo_ref[0])
    o_ref[0] = _layer_norm(alpha * x + y, lg_ref[0], lb_ref[0])


def _mix_call(x, attT, wzT, bzT, wr, br, cw, cb, clg, clb, wap, wcp, bcp, wo, lg, lb, layer, alpha):
    B, S, D = x.shape
    d_att = attT.shape[1]
    d_conv = cw.shape[-1]
    tm = TM_MIX
    lay = lambda b, i: (layer, 0, 0)
    full = lambda a: pl.BlockSpec((1,) + a.shape[1:], lay, pipeline_mode=pl.Buffered(1))
    return pl.pallas_call(
        functools.partial(_mix_kernel, d_conv=d_conv, alpha=alpha),
        grid=(B, S // tm),
        in_specs=[
            pl.BlockSpec((1, tm, D), lambda b, i: (b, i, 0)),
            pl.BlockSpec((1, HALO, D), lambda b, i: (b, jnp.maximum(i * (tm // HALO) - 1, 0), 0)),
            pl.BlockSpec((1, d_att, tm), lambda b, i: (b, 0, i)),
            full(wzT), full(bzT), full(wr), full(br), full(cw), full(cb), full(clg), full(clb),
            full(wap), full(wcp), full(bcp), full(wo), full(lg), full(lb),
        ],
        out_specs=pl.BlockSpec((1, tm, D), lambda b, i: (b, i, 0)),
        out_shape=jax.ShapeDtypeStruct((B, S, D), F32),
        scratch_shapes=[pltpu.VMEM((HALO + tm, d_conv), F32),
                        pltpu.VMEM((SUBLANES - 1, HALO + tm - SUBLANES, LANES), F32),
                        pltpu.VMEM((tm, d_conv), F32)],
        compiler_params=pltpu.CompilerParams(
            dimension_semantics=("parallel", "parallel"), vmem_limit_bytes=VMEM_LIMIT_BYTES),
        name="mix_out_norm",
    )(x, x, attT, wzT, bzT, wr, br, cw, cb, clg, clb, wap, wcp, bcp, wo, lg, lb)


def kernel(x, w_in, b_in, conv_w, conv_b, conv_ln_g, conv_ln_b, w_att_proj, w_conv_proj, b_conv_proj, w_out,
           ln_g, ln_b):
    B, S, D = x.shape
    depth = w_in.shape[0]
    d_att = w_att_proj.shape[1]
    d_conv = conv_w.shape[-1]
    n_heads = d_att // HEAD_DIM
    assert S % TM_QKV == 0 and S % TM_MIX == 0 and TM_QKV % TQ == 0 and TQ == TK
    assert n_heads % 2 == 0 and d_conv == 4 * LANES and TM_MIX % CONV_ROWS == 0
    assert n_heads >= SKEW_SUMS + SKEW_VALUES and 0 <= TAIL_STEPS <= SKEW_SUMS and CONV_WIDTH - 1 <= HALO
    assert w_in.shape[-1] == 4 * d_att + 3 * d_conv + 2 * D
    alpha = (2 * depth) ** 0.25

    o_k, o_v, o_z, o_r = d_att, 2 * d_att, 3 * d_att, 4 * d_att
    wk = w_in[:, :, o_k:o_v].astype(BF16)
    bk = b_in[:, None, o_k:o_v]
    wqvT = jnp.swapaxes(jnp.concatenate([w_in[:, :, :o_k], w_in[:, :, o_v:o_z]], axis=-1), 1, 2).astype(BF16)
    bqvT = jnp.concatenate([b_in[:, :o_k], b_in[:, o_v:o_z]], axis=-1)[:, :, None]
    wzT = jnp.swapaxes(w_in[:, :, o_z:o_r], 1, 2).astype(BF16)
    bzT = b_in[:, o_z:o_r, None]
    wr = w_in[:, :, o_r:].astype(BF16)
    br = b_in[:, None, o_r:]
    row = lambda a: a[:, None, :]
    wap, wcp, wo = w_att_proj.astype(BF16), w_conv_proj.astype(BF16), w_out.astype(BF16)

    r = lax.broadcasted_iota(jnp.int32, (TK, TK), 0)
    c = lax.broadcasted_iota(jnp.int32, (TK, TK), 1)
    ntri = jnp.where(c >= r, -1.0, 0.0).astype(BF16)

    for l in range(depth):
        k, qT, vT = _qkv_call(x, wk, bk, wqvT, bqvT, l, n_heads)
        attT = _attn_call(qT, k, vT, ntri)
        x = _mix_call(x, attT, wzT, bzT, wr, br, conv_w, row(conv_b), row(conv_ln_g), row(conv_ln_b),
                      wap, wcp, row(b_conv_proj), wo, row(ln_g), row(ln_b), l, alpha)
    return x
```

```python
import functools
import math

import jax
import jax.numpy as jnp
from jax import lax
from jax.experimental import pallas as pl
from jax.experimental.pallas import tpu as pltpu

F32 = jnp.float32
BF16 = jnp.bfloat16

HEAD_DIM = 64
CONV_WIDTH = 31
LN_EPS = 1e-5
LOG2E = math.log2(math.e)

TQ = 256
TK = 256
SKEW_SUMS = 2
SKEW_VALUES = 2
TAIL_STEPS = 2
HALO = 32
TM_QKV = 1024
TM_MIX = 512
VMEM_LIMIT_BYTES = 56 * 1024 * 1024

_NT = (((1,), (1,)), ((), ()))


def _dot(a, b):
    return jnp.dot(a, b, preferred_element_type=F32)


def _dot_nt(a, b):
    return lax.dot_general(a, b, _NT, preferred_element_type=F32)


def _sigmoid(v):
    return 0.5 * jnp.tanh(0.5 * v) + 0.5


def _silu(v):
    return v * _sigmoid(v)


def _layer_norm(v, g, b):
    mu = jnp.mean(v, axis=-1, keepdims=True)
    d = v - mu
    var = jnp.mean(d * d, axis=-1, keepdims=True)
    return d * lax.rsqrt(var + LN_EPS) * g + b


def _qkv_kernel(x_ref, wk_ref, bk_ref, wqvT_ref, bqvT_ref, k_ref, qT_ref, vT_ref, *, n_heads, q_scale):
    xb = x_ref[0].astype(BF16)
    k = _dot(xb, wk_ref[0]) + bk_ref[0]
    k_ref[0] = k.astype(BF16)
    qvT = _dot_nt(wqvT_ref[0], xb) + bqvT_ref[0]
    d_att = n_heads * HEAD_DIM
    tm = xb.shape[0]
    zeros = jnp.zeros((HEAD_DIM, TQ), BF16)
    for h in range(n_heads):
        rows = slice(h * HEAD_DIM, (h + 1) * HEAD_DIM)
        vrows = slice(d_att + h * HEAD_DIM, d_att + (h + 1) * HEAD_DIM)
        lo = (h % 2) * HEAD_DIM
        other = HEAD_DIM - lo
        for c in range(tm // TQ):
            cols = slice(c * TQ, (c + 1) * TQ)
            qT_ref[0, h, c, lo:lo + HEAD_DIM, :] = (qvT[rows, cols] * q_scale).astype(BF16)
            qT_ref[0, h, c, other:other + HEAD_DIM, :] = zeros
            vT_ref[0, h, c] = qvT[vrows, cols].astype(BF16)


def _qkv_call(x, wk, bk, wqvT, bqvT, layer, n_heads):
    B, S, D = x.shape
    d_att = n_heads * HEAD_DIM
    tm = TM_QKV
    nq = S // TQ
    lay = lambda b, i: (layer, 0, 0)
    return pl.pallas_call(
        functools.partial(_qkv_kernel, n_heads=n_heads, q_scale=HEAD_DIM ** -0.5 * LOG2E),
        grid=(B, S // tm),
        in_specs=[
            pl.BlockSpec((1, tm, D), lambda b, i: (b, i, 0)),
            pl.BlockSpec((1, D, d_att), lay),
            pl.BlockSpec((1, 1, d_att), lay),
            pl.BlockSpec((1, 2 * d_att, D), lay),
            pl.BlockSpec((1, 2 * d_att, 1), lay),
        ],
        out_specs=[
            pl.BlockSpec((1, tm, d_att), lambda b, i: (b, i, 0)),
            pl.BlockSpec((1, n_heads, tm // TQ, 2 * HEAD_DIM, TQ), lambda b, i: (b, 0, i, 0, 0)),
            pl.BlockSpec((1, n_heads, tm // TK, HEAD_DIM, TK), lambda b, i: (b, 0, i, 0, 0)),
        ],
        out_shape=[
            jax.ShapeDtypeStruct((B, S, d_att), BF16),
            jax.ShapeDtypeStruct((B, n_heads, nq, 2 * HEAD_DIM, TQ), BF16),
            jax.ShapeDtypeStruct((B, n_heads, S // TK, HEAD_DIM, TK), BF16),
        ],
        compiler_params=pltpu.CompilerParams(
            dimension_semantics=("parallel", "parallel"), vmem_limit_bytes=VMEM_LIMIT_BYTES),
        name="qkv_proj",
    )(x, wk, bk, wqvT, bqvT)


def _attn_kernel(qT_ref, k_ref, vT_ref, ntri_ref, o_ref, zbuf_ref, gbuf_ref, *, n_heads):
    qi = pl.program_id(1)
    ntri = ntri_ref[...]
    sign = jnp.uint32(0x80000000)

    def scores(h, kb):
        pair = slice((h // 2) * 2 * HEAD_DIM, (h // 2 + 1) * 2 * HEAD_DIM)
        kblk = k_ref[0, pl.ds(pl.multiple_of(kb * TK, TK), TK), pair]
        return _dot(kblk, qT_ref[0, h, 0])

    def log_fail_sums(z, mask):
        neg_abs = lax.bitcast_convert_type(lax.bitcast_convert_type(z, jnp.uint32) | sign, F32)
        p = jnp.maximum(z, 0.0) + jnp.log(1.0 + jnp.exp2(neg_abs)) * LOG2E
        if mask is not None:
            p = jnp.where(mask, p, 0.0)
        return _dot(ntri, p.astype(BF16))

    def weighted_values(h, kb, z, g, offset, mask):
        w = jnp.exp2(z + g)
        if mask is not None:
            w = jnp.where(mask, w, 0.0)
        return _dot(vT_ref[0, h, kb], w.astype(BF16)) * jnp.exp2(offset)

    def rows(h):
        return slice(h * HEAD_DIM, (h + 1) * HEAD_DIM)

    d1, d2, tail = SKEW_SUMS, SKEW_VALUES, TAIL_STEPS
    n_steps = n_heads + tail
    first_parked = n_steps - d1 - d2
    n_slots = n_heads - first_parked

    def stages(step, kb, carries, mask, cur, prev):
        z, g = cur
        offs, used = carries
        h = step - d1
        if h in z:
            g[h] = log_fail_sums(z[h], mask)
            used[h], offs[h] = offs[h], offs[h] + g[h][0:1, :]
        elif prev is not None and first_parked <= n_steps + h < n_heads:
            hp = n_steps + h
            g[h] = log_fail_sums(zbuf_ref[hp - first_parked], prev[1])
            used[hp], offs[hp] = offs[hp], offs[hp] + g[h][0:1, :]
        h = step - d1 - d2
        if h in z:
            pv = weighted_values(h, kb, z.pop(h), g.pop(h), used[h], mask)
            if prev is None:
                o_ref[0, rows(h), :] = pv
            else:
                o_ref[0, rows(h), :] += pv
        elif prev is not None and first_parked <= n_steps + h < n_heads:
            hp = n_steps + h
            slot = hp - first_parked
            gp = g.pop(h) if h in g else gbuf_ref[slot]
            o_ref[0, rows(hp), :] += weighted_values(hp, prev[0], zbuf_ref[slot], gp, used[hp], prev[1])

    def key_block(kb, carries, mask, prev):
        carries = (list(carries[0]), list(carries[1]))
        z, g = {}, {}
        for step in range(n_steps):
            if step < n_heads:
                z[step] = scores(step, kb)
            stages(step, kb, carries, mask, (z, g), prev)
        for slot in range(n_slots):
            h = first_parked + slot
            zbuf_ref[slot] = z[h]
            if h in g:
                gbuf_ref[slot] = g[h]
        return tuple(carries[0]), tuple(carries[1])

    key_pos = lax.broadcasted_iota(jnp.int32, (TK, TQ), 0)
    qry_pos = lax.broadcasted_iota(jnp.int32, (TK, TQ), 1)
    diag_mask = key_pos < qry_pos
    zero = jnp.zeros((1, TQ), F32)
    for h in range(first_parked, n_heads):
        o_ref[0, rows(h), :] = jnp.zeros((HEAD_DIM, TQ), F32)
    carries = key_block(qi, ((zero,) * n_heads, (zero,) * n_heads), diag_mask, None)
    before = jnp.broadcast_to(qi > 0, (TK, TQ))
    carries = key_block(jnp.maximum(qi - 1, 0), carries, before, (qi, diag_mask))

    def any_weight_left(offs):
        top = functools.reduce(jnp.maximum, offs)
        return (jnp.max(jnp.exp2(top)) > 0.0).astype(jnp.int32)

    def more_blocks(state):
        i, alive, _, _ = state
        return jnp.logical_and(i < qi, alive > 0)

    def body(state):
        i, _, offs, used = state
        offs, used = key_block(qi - 1 - i, (offs, used), None, (qi - i, None))
        return i + 1, any_weight_left(offs), offs, used

    n_done, _, offs, used = lax.while_loop(
        more_blocks, body, (jnp.int32(1), any_weight_left(carries[0])) + carries)
    carries = (list(offs), list(used))
    last = jnp.maximum(qi - n_done, 0)
    g = {}
    for step in range(n_slots):
        stages(step, last, carries, None, ({}, g), (last, before))


def _attn_call(qT, k, vT, ntri):
    B, H, nq = qT.shape[:3]
    S = k.shape[1]
    return pl.pallas_call(
        functools.partial(_attn_kernel, n_heads=H),
        grid=(B, nq),
        in_specs=[
            pl.BlockSpec((1, H, 1, 2 * HEAD_DIM, TQ), lambda b, q: (b, 0, q, 0, 0)),
            pl.BlockSpec((1, S, H * HEAD_DIM), lambda b, q: (b, 0, 0)),
            pl.BlockSpec((1, H, S // TK, HEAD_DIM, TK), lambda b, q: (b, 0, 0, 0, 0)),
            pl.BlockSpec((TK, TK), lambda b, q: (0, 0)),
        ],
        out_specs=pl.BlockSpec((1, H * HEAD_DIM, TQ), lambda b, q: (b, 0, q)),
        out_shape=jax.ShapeDtypeStruct((B, H * HEAD_DIM, S), F32),
        scratch_shapes=[pltpu.VMEM((SKEW_SUMS + SKEW_VALUES - TAIL_STEPS, TK, TQ), F32),
                        pltpu.VMEM((SKEW_VALUES, TK, TQ), F32)],
        compiler_params=pltpu.CompilerParams(
            dimension_semantics=("parallel", "parallel"), vmem_limit_bytes=VMEM_LIMIT_BYTES),
        name="stickbreak_attn",
    )(qT, k, vT, ntri)


CONV_ROWS = 64
LANES = 128
SUBLANES = 8


def _mix_kernel(x_ref, xh_ref, attT_ref, wzT_ref, bzT_ref, wr_ref, br_ref, cw_ref, cb_ref, clg_ref, clb_ref,
                wap_ref, wcp_ref, bcp_ref, wo_ref, lg_ref, lb_ref, o_ref, cext_ref, shifted_ref, conv_ref, *,
                d_conv, alpha):
    i = pl.program_id(1)
    x = x_ref[0]
    xb = x.astype(BF16)
    tm, d_model = x.shape
    o_ga, o_gb, o_zc, o_gat, o_gcv, o_end = (0, d_conv, 2 * d_conv, 3 * d_conv, 3 * d_conv + d_model,
                                            3 * d_conv + 2 * d_model)

    def proj(lhs, lo, hi):
        return _dot(lhs, wr_ref[0, :, lo:hi]) + br_ref[0, :, lo:hi]

    xhb = xh_ref[0].astype(BF16)
    half = d_conv // 2
    for lo in (0, half):
        a0, b0 = o_ga + lo, o_gb + lo
        c = proj(xb, a0, a0 + half) * _sigmoid(proj(xb, b0, b0 + half))
        ch = proj(xhb, a0, a0 + half) * _sigmoid(proj(xhb, b0, b0 + half))
        cext_ref[0:HALO, lo:lo + half] = jnp.where(i > 0, ch, 0.0)
        cext_ref[HALO:, lo:lo + half] = c

    gate_att = proj(xb, o_gat, o_gcv)
    z_conv = proj(xb, o_zc, o_gat)
    gate_conv = proj(xb, o_gcv, o_end)
    zaT = _dot_nt(wzT_ref[0], xb) + bzT_ref[0]

    lead = HALO - (CONV_WIDTH - 1)
    span = tm + HALO - SUBLANES
    for lb in range(d_conv // LANES):
        lanes = slice(lb * LANES, (lb + 1) * LANES)
        for r in range(1, SUBLANES):
            shifted_ref[r - 1, :, :] = cext_ref[r:r + span, lanes]
        for rc in range(tm // CONV_ROWS):
            acc = jnp.broadcast_to(cb_ref[0, :, lanes], (CONV_ROWS, LANES))
            for j in range(CONV_WIDTH):
                r, r0 = (j + lead) % SUBLANES, rc * CONV_ROWS + (j + lead) // SUBLANES * SUBLANES
                if r == 0:
                    taps = cext_ref[r0:r0 + CONV_ROWS, lanes]
                else:
                    taps = shifted_ref[r - 1, r0:r0 + CONV_ROWS, :]
                acc = acc + cw_ref[0, j:j + 1, lanes] * taps
            conv_ref[rc * CONV_ROWS:(rc + 1) * CONV_ROWS, lanes] = acc

    gT = attT_ref[0] * _silu(zaT)
    att_branch = _dot(gT.T.astype(BF16), wap_ref[0])
    cg = _silu(_layer_norm(conv_ref[...], clg_ref[0], clb_ref[0])) * _silu(z_conv)
    conv_branch = _dot(cg.astype(BF16), wcp_ref[0]) + bcp_ref[0]
    merged = _sigmoid(gate_att) * att_branch + _sigmoid(gate_conv) * conv_branch
    y = _dot(merged.astype(BF16), wo_ref[0])
    o_ref[0] = _layer_norm(alpha * x + y, lg_ref[0], lb_ref[0])


def _mix_call(x, attT, wzT, bzT, wr, br, cw, cb, clg, clb, wap, wcp, bcp, wo, lg, lb, layer, alpha):
    B, S, D = x.shape
    d_att = attT.shape[1]
    d_conv = cw.shape[-1]
    tm = TM_MIX
    lay = lambda b, i: (layer, 0, 0)
    full = lambda a: pl.BlockSpec((1,) + a.shape[1:], lay, pipeline_mode=pl.Buffered(1))
    return pl.pallas_call(
        functools.partial(_mix_kernel, d_conv=d_conv, alpha=alpha),
        grid=(B, S // tm),
        in_specs=[
            pl.BlockSpec((1, tm, D), lambda b, i: (b, i, 0)),
            pl.BlockSpec((1, HALO, D), lambda b, i: (b, jnp.maximum(i * (tm // HALO) - 1, 0), 0)),
            pl.BlockSpec((1, d_att, tm), lambda b, i: (b, 0, i)),
            full(wzT), full(bzT), full(wr), full(br), full(cw), full(cb), full(clg), full(clb),
            full(wap), full(wcp), full(bcp), full(wo), full(lg), full(lb),
        ],
        out_specs=pl.BlockSpec((1, tm, D), lambda b, i: (b, i, 0)),
        out_shape=jax.ShapeDtypeStruct((B, S, D), F32),
        scratch_shapes=[pltpu.VMEM((HALO + tm, d_conv), F32),
                        pltpu.VMEM((SUBLANES - 1, HALO + tm - SUBLANES, LANES), F32),
                        pltpu.VMEM((tm, d_conv), F32)],
        compiler_params=pltpu.CompilerParams(
            dimension_semantics=("parallel", "parallel"), vmem_limit_bytes=VMEM_LIMIT_BYTES),
        name="mix_out_norm",
    )(x, x, attT, wzT, bzT, wr, br, cw, cb, clg, clb, wap, wcp, bcp, wo, lg, lb)


def kernel(x, w_in, b_in, conv_w, conv_b, conv_ln_g, conv_ln_b, w_att_proj, w_conv_proj, b_conv_proj, w_out,
           ln_g, ln_b):
    B, S, D = x.shape
    depth = w_in.shape[0]
    d_att = w_att_proj.shape[1]
    d_conv = conv_w.shape[-1]
    n_heads = d_att // HEAD_DIM
    assert S % TM_QKV == 0 and S % TM_MIX == 0 and TM_QKV % TQ == 0 and TQ == TK
    assert n_heads % 2 == 0 and d_conv == 4 * LANES and TM_MIX % CONV_ROWS == 0
    assert n_heads >= SKEW_SUMS + SKEW_VALUES and 0 <= TAIL_STEPS <= SKEW_SUMS and CONV_WIDTH - 1 <= HALO
    assert w_in.shape[-1] == 4 * d_att + 3 * d_conv + 2 * D
    alpha = (2 * depth) ** 0.25

    o_k, o_v, o_z, o_r = d_att, 2 * d_att, 3 * d_att, 4 * d_att
    wk = w_in[:, :, o_k:o_v].astype(BF16)
    bk = b_in[:, None, o_k:o_v]
    wqvT = jnp.swapaxes(jnp.concatenate([w_in[:, :, :o_k], w_in[:, :, o_v:o_z]], axis=-1), 1, 2).astype(BF16)
    bqvT = jnp.concatenate([b_in[:, :o_k], b_in[:, o_v:o_z]], axis=-1)[:, :, None]
    wzT = jnp.swapaxes(w_in[:, :, o_z:o_r], 1, 2).astype(BF16)
    bzT = b_in[:, o_z:o_r, None]
    wr = w_in[:, :, o_r:].astype(BF16)
    br = b_in[:, None, o_r:]
    row = lambda a: a[:, None, :]
    wap, wcp, wo = w_att_proj.astype(BF16), w_conv_proj.astype(BF16), w_out.astype(BF16)

    r = lax.broadcasted_iota(jnp.int32, (TK, TK), 0)
    c = lax.broadcasted_iota(jnp.int32, (TK, TK), 1)
    ntri = jnp.where(c >= r, -1.0, 0.0).astype(BF16)

    for l in range(depth):
        k, qT, vT = _qkv_call(x, wk, bk, wqvT, bqvT, l, n_heads)
        attT = _attn_call(qT, k, vT, ntri)
        x = _mix_call(x, attT, wzT, bzT, wr, br, conv_w, row(conv_b), row(conv_ln_g), row(conv_ln_b),
                      wap, wcp, row(b_conv_proj), wo, row(ln_g), row(ln_b), l, alpha)
    return x
```

```python
import functools
import math

import jax
import jax.numpy as jnp
from jax import lax
from jax.experimental import pallas as pl
from jax.experimental.pallas import tpu as pltpu

F32 = jnp.float32
BF16 = jnp.bfloat16

HEAD_DIM = 64
CONV_WIDTH = 31
LN_EPS = 1e-5
LOG2E = math.log2(math.e)

TQ = 256
TK = 256
SKEW_SUMS = 3
SKEW_VALUES = 2
TAIL_STEPS = 3
HALO = 32
TM_QKV = 1024
TM_MIX = 512
VMEM_LIMIT_BYTES = 56 * 1024 * 1024

_NT = (((1,), (1,)), ((), ()))


def _dot(a, b):
    return jnp.dot(a, b, preferred_element_type=F32)


def _dot_nt(a, b):
    return lax.dot_general(a, b, _NT, preferred_element_type=F32)


def _sigmoid(v):
    return 0.5 * jnp.tanh(0.5 * v) + 0.5


def _silu(v):
    return v * _sigmoid(v)


def _layer_norm(v, g, b):
    mu = jnp.mean(v, axis=-1, keepdims=True)
    d = v - mu
    var = jnp.mean(d * d, axis=-1, keepdims=True)
    return d * lax.rsqrt(var + LN_EPS) * g + b


def _qkv_kernel(x_ref, wk_ref, bk_ref, wqvT_ref, bqvT_ref, k_ref, qT_ref, vT_ref, *, n_heads, q_scale):
    xb = x_ref[0].astype(BF16)
    k = _dot(xb, wk_ref[0]) + bk_ref[0]
    k_ref[0] = k.astype(BF16)
    qvT = _dot_nt(wqvT_ref[0], xb) + bqvT_ref[0]
    d_att = n_heads * HEAD_DIM
    tm = xb.shape[0]
    zeros = jnp.zeros((HEAD_DIM, TQ), BF16)
    for h in range(n_heads):
        rows = slice(h * HEAD_DIM, (h + 1) * HEAD_DIM)
        vrows = slice(d_att + h * HEAD_DIM, d_att + (h + 1) * HEAD_DIM)
        lo = (h % 2) * HEAD_DIM
        other = HEAD_DIM - lo
        for c in range(tm // TQ):
            cols = slice(c * TQ, (c + 1) * TQ)
            qT_ref[0, h, c, lo:lo + HEAD_DIM, :] = (qvT[rows, cols] * q_scale).astype(BF16)
            qT_ref[0, h, c, other:other + HEAD_DIM, :] = zeros
            vT_ref[0, h, c] = qvT[vrows, cols].astype(BF16)


def _qkv_call(x, wk, bk, wqvT, bqvT, layer, n_heads):
    B, S, D = x.shape
    d_att = n_heads * HEAD_DIM
    tm = TM_QKV
    nq = S // TQ
    lay = lambda b, i: (layer, 0, 0)
    return pl.pallas_call(
        functools.partial(_qkv_kernel, n_heads=n_heads, q_scale=HEAD_DIM ** -0.5 * LOG2E),
        grid=(B, S // tm),
        in_specs=[
            pl.BlockSpec((1, tm, D), lambda b, i: (b, i, 0)),
            pl.BlockSpec((1, D, d_att), lay),
            pl.BlockSpec((1, 1, d_att), lay),
            pl.BlockSpec((1, 2 * d_att, D), lay),
            pl.BlockSpec((1, 2 * d_att, 1), lay),
        ],
        out_specs=[
            pl.BlockSpec((1, tm, d_att), lambda b, i: (b, i, 0)),
            pl.BlockSpec((1, n_heads, tm // TQ, 2 * HEAD_DIM, TQ), lambda b, i: (b, 0, i, 0, 0)),
            pl.BlockSpec((1, n_heads, tm // TK, HEAD_DIM, TK), lambda b, i: (b, 0, i, 0, 0)),
        ],
        out_shape=[
            jax.ShapeDtypeStruct((B, S, d_att), BF16),
            jax.ShapeDtypeStruct((B, n_heads, nq, 2 * HEAD_DIM, TQ), BF16),
            jax.ShapeDtypeStruct((B, n_heads, S // TK, HEAD_DIM, TK), BF16),
        ],
        compiler_params=pltpu.CompilerParams(
            dimension_semantics=("parallel", "parallel"), vmem_limit_bytes=VMEM_LIMIT_BYTES),
        name="qkv_proj",
    )(x, wk, bk, wqvT, bqvT)


def _attn_kernel(qT_ref, k_ref, vT_ref, ntri_ref, o_ref, zbuf_ref, gbuf_ref, *, n_heads):
    qi = pl.program_id(1)
    ntri = ntri_ref[...]
    sign = jnp.uint32(0x80000000)

    def scores(h, kb):
        pair = slice((h // 2) * 2 * HEAD_DIM, (h // 2 + 1) * 2 * HEAD_DIM)
        kblk = k_ref[0, pl.ds(pl.multiple_of(kb * TK, TK), TK), pair]
        return _dot(kblk, qT_ref[0, h, 0])

    def log_fail_sums(z, mask):
        neg_abs = lax.bitcast_convert_type(lax.bitcast_convert_type(z, jnp.uint32) | sign, F32)
        p = jnp.maximum(z, 0.0) + jnp.log(1.0 + jnp.exp2(neg_abs)) * LOG2E
        if mask is not None:
            p = jnp.where(mask, p, 0.0)
        return _dot(ntri, p.astype(BF16))

    def weighted_values(h, kb, z, g, offset, mask):
        w = jnp.exp2(z + g)
        if mask is not None:
            w = jnp.where(mask, w, 0.0)
        return _dot(vT_ref[0, h, kb], w.astype(BF16)) * jnp.exp2(offset)

    def rows(h):
        return slice(h * HEAD_DIM, (h + 1) * HEAD_DIM)

    d1, d2, tail = SKEW_SUMS, SKEW_VALUES, TAIL_STEPS
    n_steps = n_heads + tail
    first_parked = n_steps - d1 - d2
    n_slots = n_heads - first_parked

    def stages(step, kb, carries, mask, cur, prev):
        z, g = cur
        offs, used = carries
        h = step - d1
        if h in z:
            g[h] = log_fail_sums(z[h], mask)
            used[h], offs[h] = offs[h], offs[h] + g[h][0:1, :]
        elif prev is not None and first_parked <= n_steps + h < n_heads:
            hp = n_steps + h
            g[h] = log_fail_sums(zbuf_ref[hp - first_parked], prev[1])
            used[hp], offs[hp] = offs[hp], offs[hp] + g[h][0:1, :]
        h = step - d1 - d2
        if h in z:
            pv = weighted_values(h, kb, z.pop(h), g.pop(h), used[h], mask)
            if prev is None:
                o_ref[0, rows(h), :] = pv
            else:
                o_ref[0, rows(h), :] += pv
        elif prev is not None and first_parked <= n_steps + h < n_heads:
            hp = n_steps + h
            slot = hp - first_parked
            gp = g.pop(h) if h in g else gbuf_ref[slot]
            o_ref[0, rows(hp), :] += weighted_values(hp, prev[0], zbuf_ref[slot], gp, used[hp], prev[1])

    def key_block(kb, carries, mask, prev):
        carries = (list(carries[0]), list(carries[1]))
        z, g = {}, {}
        for step in range(n_steps):
            if step < n_heads:
                z[step] = scores(step, kb)
            stages(step, kb, carries, mask, (z, g), prev)
        for slot in range(n_slots):
            h = first_parked + slot
            zbuf_ref[slot] = z[h]
            if h in g:
                gbuf_ref[slot] = g[h]
        return tuple(carries[0]), tuple(carries[1])

    key_pos = lax.broadcasted_iota(jnp.int32, (TK, TQ), 0)
    qry_pos = lax.broadcasted_iota(jnp.int32, (TK, TQ), 1)
    diag_mask = key_pos < qry_pos
    zero = jnp.zeros((1, TQ), F32)
    for h in range(first_parked, n_heads):
        o_ref[0, rows(h), :] = jnp.zeros((HEAD_DIM, TQ), F32)
    carries = key_block(qi, ((zero,) * n_heads, (zero,) * n_heads), diag_mask, None)
    before = jnp.broadcast_to(qi > 0, (TK, TQ))
    carries = key_block(jnp.maximum(qi - 1, 0), carries, before, (qi, diag_mask))

    def any_weight_left(offs):
        top = functools.reduce(jnp.maximum, offs)
        return (jnp.max(jnp.exp2(top)) > 0.0).astype(jnp.int32)

    def more_blocks(state):
        i, alive, _, _ = state
        return jnp.logical_and(i < qi, alive > 0)

    def body(state):
        i, _, offs, used = state
        offs, used = key_block(qi - 1 - i, (offs, used), None, (qi - i, None))
        return i + 1, any_weight_left(offs), offs, used

    n_done, _, offs, used = lax.while_loop(
        more_blocks, body, (jnp.int32(1), any_weight_left(carries[0])) + carries)
    carries = (list(offs), list(used))
    last = jnp.maximum(qi - n_done, 0)
    g = {}
    for step in range(n_slots):
        stages(step, last, carries, None, ({}, g), (last, before))


def _attn_call(qT, k, vT, ntri):
    B, H, nq = qT.shape[:3]
    S = k.shape[1]
    return pl.pallas_call(
        functools.partial(_attn_kernel, n_heads=H),
        grid=(B, nq),
        in_specs=[
            pl.BlockSpec((1, H, 1, 2 * HEAD_DIM, TQ), lambda b, q: (b, 0, q, 0, 0)),
            pl.BlockSpec((1, S, H * HEAD_DIM), lambda b, q: (b, 0, 0)),
            pl.BlockSpec((1, H, S // TK, HEAD_DIM, TK), lambda b, q: (b, 0, 0, 0, 0)),
            pl.BlockSpec((TK, TK), lambda b, q: (0, 0)),
        ],
        out_specs=pl.BlockSpec((1, H * HEAD_DIM, TQ), lambda b, q: (b, 0, q)),
        out_shape=jax.ShapeDtypeStruct((B, H * HEAD_DIM, S), F32),
        scratch_shapes=[pltpu.VMEM((SKEW_SUMS + SKEW_VALUES - TAIL_STEPS, TK, TQ), F32),
                        pltpu.VMEM((SKEW_VALUES, TK, TQ), F32)],
        compiler_params=pltpu.CompilerParams(
            dimension_semantics=("parallel", "parallel"), vmem_limit_bytes=VMEM_LIMIT_BYTES),
        name="stickbreak_attn",
    )(qT, k, vT, ntri)


CONV_ROWS = 64
LANES = 128
SUBLANES = 8


def _mix_kernel(x_ref, xh_ref, attT_ref, wzT_ref, bzT_ref, wr_ref, br_ref, cw_ref, cb_ref, clg_ref, clb_ref,
                wap_ref, wcp_ref, bcp_ref, wo_ref, lg_ref, lb_ref, o_ref, cext_ref, shifted_ref, conv_ref, *,
                d_conv, alpha):
    i = pl.program_id(1)
    x = x_ref[0]
    xb = x.astype(BF16)
    tm, d_model = x.shape
    o_ga, o_gb, o_zc, o_gat, o_gcv, o_end = (0, d_conv, 2 * d_conv, 3 * d_conv, 3 * d_conv + d_model,
                                            3 * d_conv + 2 * d_model)

    def proj(lhs, lo, hi):
        return _dot(lhs, wr_ref[0, :, lo:hi]) + br_ref[0, :, lo:hi]

    xcat = jnp.concatenate([xh_ref[0].astype(BF16), xb], axis=0)
    half = d_conv // 2
    for lo in (0, half):
        a0, b0 = o_ga + lo, o_gb + lo
        c = proj(xcat, a0, a0 + half) * _sigmoid(proj(xcat, b0, b0 + half))
        cext_ref[0:HALO, lo:lo + half] = jnp.where(i > 0, c[0:HALO], 0.0)
        cext_ref[HALO:, lo:lo + half] = c[HALO:]

    gate_att = proj(xb, o_gat, o_gcv)
    z_conv = proj(xb, o_zc, o_gat)
    gate_conv = proj(xb, o_gcv, o_end)
    zaT = _dot_nt(wzT_ref[0], xb) + bzT_ref[0]

    lead = HALO - (CONV_WIDTH - 1)
    span = tm + HALO - SUBLANES
    for lb in range(d_conv // LANES):
        lanes = slice(lb * LANES, (lb + 1) * LANES)
        for r in range(1, SUBLANES):
            shifted_ref[r - 1, :, :] = cext_ref[r:r + span, lanes]
        for rc in range(tm // CONV_ROWS):
            acc = jnp.broadcast_to(cb_ref[0, :, lanes], (CONV_ROWS, LANES))
            for j in range(CONV_WIDTH):
                r, r0 = (j + lead) % SUBLANES, rc * CONV_ROWS + (j + lead) // SUBLANES * SUBLANES
                if r == 0:
                    taps = cext_ref[r0:r0 + CONV_ROWS, lanes]
                else:
                    taps = shifted_ref[r - 1, r0:r0 + CONV_ROWS, :]
                acc = acc + cw_ref[0, j:j + 1, lanes] * taps
            conv_ref[rc * CONV_ROWS:(rc + 1) * CONV_ROWS, lanes] = acc

    gT = attT_ref[0] * _silu(zaT)
    att_branch = _dot(gT.T.astype(BF16), wap_ref[0])
    cg = _silu(_layer_norm(conv_ref[...], clg_ref[0], clb_ref[0])) * _silu(z_conv)
    conv_branch = _dot(cg.astype(BF16), wcp_ref[0]) + bcp_ref[0]
    merged = _sigmoid(gate_att) * att_branch + _sigmoid(gate_conv) * conv_branch
    mb = merged.astype(BF16)
    for r0 in range(0, tm, tm // 2):
        r1 = r0 + tm // 2
        y = _dot(mb[r0:r1], wo_ref[0])
        o_ref[0, r0:r1, :] = _layer_norm(alpha * x[r0:r1] + y, lg_ref[0], lb_ref[0])


def _mix_call(x, attT, wzT, bzT, wr, br, cw, cb, clg, clb, wap, wcp, bcp, wo, lg, lb, layer, alpha):
    B, S, D = x.shape
    d_att = attT.shape[1]
    d_conv = cw.shape[-1]
    tm = TM_MIX
    lay = lambda b, i: (layer, 0, 0)
    full = lambda a: pl.BlockSpec((1,) + a.shape[1:], lay, pipeline_mode=pl.Buffered(1))
    return pl.pallas_call(
        functools.partial(_mix_kernel, d_conv=d_conv, alpha=alpha),
        grid=(B, S // tm),
        in_specs=[
            pl.BlockSpec((1, tm, D), lambda b, i: (b, i, 0)),
            pl.BlockSpec((1, HALO, D), lambda b, i: (b, jnp.maximum(i * (tm // HALO) - 1, 0), 0)),
            pl.BlockSpec((1, d_att, tm), lambda b, i: (b, 0, i)),
            full(wzT), full(bzT), full(wr), full(br), full(cw), full(cb), full(clg), full(clb),
            full(wap), full(wcp), full(bcp), full(wo), full(lg), full(lb),
        ],
        out_specs=pl.BlockSpec((1, tm, D), lambda b, i: (b, i, 0)),
        out_shape=jax.ShapeDtypeStruct((B, S, D), F32),
        scratch_shapes=[pltpu.VMEM((HALO + tm, d_conv), F32),
                        pltpu.VMEM((SUBLANES - 1, HALO + tm - SUBLANES, LANES), F32),
                        pltpu.VMEM((tm, d_conv), F32)],
        compiler_params=pltpu.CompilerParams(
            dimension_semantics=("parallel", "parallel"), vmem_limit_bytes=VMEM_LIMIT_BYTES),
        name="mix_out_norm",
    )(x, x, attT, wzT, bzT, wr, br, cw, cb, clg, clb, wap, wcp, bcp, wo, lg, lb)


def kernel(x, w_in, b_in, conv_w, conv_b, conv_ln_g, conv_ln_b, w_att_proj, w_conv_proj, b_conv_proj, w_out,
           ln_g, ln_b):
    B, S, D = x.shape
    depth = w_in.shape[0]
    d_att = w_att_proj.shape[1]
    d_conv = conv_w.shape[-1]
    n_heads = d_att // HEAD_DIM
    assert S % TM_QKV == 0 and S % TM_MIX == 0 and TM_QKV % TQ == 0 and TQ == TK
    assert n_heads % 2 == 0 and d_conv == 4 * LANES and TM_MIX % CONV_ROWS == 0
    assert n_heads >= SKEW_SUMS + SKEW_VALUES and 0 <= TAIL_STEPS <= SKEW_SUMS and CONV_WIDTH - 1 <= HALO
    assert w_in.shape[-1] == 4 * d_att + 3 * d_conv + 2 * D
    alpha = (2 * depth) ** 0.25

    o_k, o_v, o_z, o_r = d_att, 2 * d_att, 3 * d_att, 4 * d_att
    wk = w_in[:, :, o_k:o_v].astype(BF16)
    bk = b_in[:, None, o_k:o_v]
    wqvT = jnp.swapaxes(jnp.concatenate([w_in[:, :, :o_k], w_in[:, :, o_v:o_z]], axis=-1), 1, 2).astype(BF16)
    bqvT = jnp.concatenate([b_in[:, :o_k], b_in[:, o_v:o_z]], axis=-1)[:, :, None]
    wzT = jnp.swapaxes(w_in[:, :, o_z:o_r], 1, 2).astype(BF16)
    bzT = b_in[:, o_z:o_r, None]
    wr = w_in[:, :, o_r:].astype(BF16)
    br = b_in[:, None, o_r:]
    row = lambda a: a[:, None, :]
    wap, wcp, wo = w_att_proj.astype(BF16), w_conv_proj.astype(BF16), w_out.astype(BF16)

    r = lax.broadcasted_iota(jnp.int32, (TK, TK), 0)
    c = lax.broadcasted_iota(jnp.int32, (TK, TK), 1)
    ntri = jnp.where(c >= r, -1.0, 0.0).astype(BF16)

    for l in range(depth):
        k, qT, vT = _qkv_call(x, wk, bk, wqvT, bqvT, l, n_heads)
        attT = _attn_call(qT, k, vT, ntri)
        x = _mix_call(x, attT, wzT, bzT, wr, br, conv_w, row(conv_b), row(conv_ln_g), row(conv_ln_b),
                      wap, wcp, row(b_conv_proj), wo, row(ln_g), row(ln_b), l, alpha)
    return x
```

```python
import functools
import math

import jax
import jax.numpy as jnp
from jax import lax
from jax.experimental import pallas as pl
from jax.experimental.pallas import tpu as pltpu

F32 = jnp.float32
BF16 = jnp.bfloat16

HEAD_DIM = 64
CONV_WIDTH = 31
LN_EPS = 1e-5
LOG2E = math.log2(math.e)

TQ = 256
TK = 256
SKEW_SUMS = 3
SKEW_VALUES = 2
TAIL_STEPS = 3
HALO = 32
TM_QKV = 1024
Q_BLOCKS_PER_STEP = 2
TM_MIX = 512
VMEM_LIMIT_BYTES = 56 * 1024 * 1024

_NT = (((1,), (1,)), ((), ()))


def _dot(a, b):
    return jnp.dot(a, b, preferred_element_type=F32)


def _dot_nt(a, b):
    return lax.dot_general(a, b, _NT, preferred_element_type=F32)


def _sigmoid(v):
    return 0.5 * jnp.tanh(0.5 * v) + 0.5


def _silu(v):
    return v * _sigmoid(v)


def _layer_norm(v, g, b):
    mu = jnp.mean(v, axis=-1, keepdims=True)
    d = v - mu
    var = jnp.mean(d * d, axis=-1, keepdims=True)
    return d * lax.rsqrt(var + LN_EPS) * g + b


def _qkv_kernel(x_ref, wk_ref, bk_ref, wqvT_ref, bqvT_ref, k_ref, qT_ref, vT_ref, *, n_heads, q_scale):
    xb = x_ref[0].astype(BF16)
    k = _dot(xb, wk_ref[0]) + bk_ref[0]
    k_ref[0] = k.astype(BF16)
    qvT = _dot_nt(wqvT_ref[0], xb) + bqvT_ref[0]
    d_att = n_heads * HEAD_DIM
    tm = xb.shape[0]
    zeros = jnp.zeros((HEAD_DIM, TQ), BF16)
    for h in range(n_heads):
        rows = slice(h * HEAD_DIM, (h + 1) * HEAD_DIM)
        vrows = slice(d_att + h * HEAD_DIM, d_att + (h + 1) * HEAD_DIM)
        lo = (h % 2) * HEAD_DIM
        other = HEAD_DIM - lo
        for c in range(tm // TQ):
            cols = slice(c * TQ, (c + 1) * TQ)
            qT_ref[0, h, c, lo:lo + HEAD_DIM, :] = (qvT[rows, cols] * q_scale).astype(BF16)
            qT_ref[0, h, c, other:other + HEAD_DIM, :] = zeros
            vT_ref[0, h, c] = qvT[vrows, cols].astype(BF16)


def _qkv_call(x, wk, bk, wqvT, bqvT, layer, n_heads):
    B, S, D = x.shape
    d_att = n_heads * HEAD_DIM
    tm = TM_QKV
    nq = S // TQ
    lay = lambda b, i: (layer, 0, 0)
    return pl.pallas_call(
        functools.partial(_qkv_kernel, n_heads=n_heads, q_scale=HEAD_DIM ** -0.5 * LOG2E),
        grid=(B, S // tm),
        in_specs=[
            pl.BlockSpec((1, tm, D), lambda b, i: (b, i, 0)),
            pl.BlockSpec((1, D, d_att), lay),
            pl.BlockSpec((1, 1, d_att), lay),
            pl.BlockSpec((1, 2 * d_att, D), lay),
            pl.BlockSpec((1, 2 * d_att, 1), lay),
        ],
        out_specs=[
            pl.BlockSpec((1, tm, d_att), lambda b, i: (b, i, 0)),
            pl.BlockSpec((1, n_heads, tm // TQ, 2 * HEAD_DIM, TQ), lambda b, i: (b, 0, i, 0, 0)),
            pl.BlockSpec((1, n_heads, tm // TK, HEAD_DIM, TK), lambda b, i: (b, 0, i, 0, 0)),
        ],
        out_shape=[
            jax.ShapeDtypeStruct((B, S, d_att), BF16),
            jax.ShapeDtypeStruct((B, n_heads, nq, 2 * HEAD_DIM, TQ), BF16),
            jax.ShapeDtypeStruct((B, n_heads, S // TK, HEAD_DIM, TK), BF16),
        ],
        compiler_params=pltpu.CompilerParams(
            dimension_semantics=("parallel", "parallel"), vmem_limit_bytes=VMEM_LIMIT_BYTES),
        name="qkv_proj",
    )(x, wk, bk, wqvT, bqvT)


def _attn_kernel(qT_ref, k_ref, vT_ref, ntri_ref, o_ref, zbuf_ref, gbuf_ref, *, n_heads):
    for sub in range(Q_BLOCKS_PER_STEP):
        _attn_query_block(pl.program_id(1) * Q_BLOCKS_PER_STEP + sub, sub, qT_ref, k_ref, vT_ref, ntri_ref,
                          o_ref, zbuf_ref, gbuf_ref, n_heads)


def _attn_query_block(qi, sub, qT_ref, k_ref, vT_ref, ntri_ref, o_ref, zbuf_ref, gbuf_ref, n_heads):
    cols = slice(sub * TQ, (sub + 1) * TQ)
    ntri = ntri_ref[...]
    sign = jnp.uint32(0x80000000)

    def scores(h, kb):
        pair = slice((h // 2) * 2 * HEAD_DIM, (h // 2 + 1) * 2 * HEAD_DIM)
        kblk = k_ref[0, pl.ds(pl.multiple_of(kb * TK, TK), TK), pair]
        return _dot(kblk, qT_ref[0, h, sub])

    def log_fail_sums(z, mask):
        neg_abs = lax.bitcast_convert_type(lax.bitcast_convert_type(z, jnp.uint32) | sign, F32)
        p = jnp.maximum(z, 0.0) + jnp.log(1.0 + jnp.exp2(neg_abs)) * LOG2E
        if mask is not None:
            p = jnp.where(mask, p, 0.0)
        return _dot(ntri, p.astype(BF16))

    def weighted_values(h, kb, z, g, offset, mask):
        w = jnp.exp2(z + g)
        if mask is not None:
            w = jnp.where(mask, w, 0.0)
        return _dot(vT_ref[0, h, kb], w.astype(BF16)) * jnp.exp2(offset)

    def rows(h):
        return slice(h * HEAD_DIM, (h + 1) * HEAD_DIM)

    d1, d2, tail = SKEW_SUMS, SKEW_VALUES, TAIL_STEPS
    n_steps = n_heads + tail
    first_parked = n_steps - d1 - d2
    n_slots = n_heads - first_parked

    def stages(step, kb, carries, mask, cur, prev):
        z, g = cur
        offs, used = carries
        h = step - d1
        if h in z:
            g[h] = log_fail_sums(z[h], mask)
            used[h], offs[h] = offs[h], offs[h] + g[h][0:1, :]
        elif prev is not None and first_parked <= n_steps + h < n_heads:
            hp = n_steps + h
            g[h] = log_fail_sums(zbuf_ref[hp - first_parked], prev[1])
            used[hp], offs[hp] = offs[hp], offs[hp] + g[h][0:1, :]
        h = step - d1 - d2
        if h in z:
            pv = weighted_values(h, kb, z.pop(h), g.pop(h), used[h], mask)
            if prev is None:
                o_ref[0, rows(h), cols] = pv
            else:
                o_ref[0, rows(h), cols] += pv
        elif prev is not None and first_parked <= n_steps + h < n_heads:
            hp = n_steps + h
            slot = hp - first_parked
            gp = g.pop(h) if h in g else gbuf_ref[slot]
            o_ref[0, rows(hp), cols] += weighted_values(hp, prev[0], zbuf_ref[slot], gp, used[hp], prev[1])

    def key_block(kb, carries, mask, prev):
        carries = (list(carries[0]), list(carries[1]))
        z, g = {}, {}
        for step in range(n_steps):
            if step < n_heads:
                z[step] = scores(step, kb)
            stages(step, kb, carries, mask, (z, g), prev)
        for slot in range(n_slots):
            h = first_parked + slot
            zbuf_ref[slot] = z[h]
            if h in g:
                gbuf_ref[slot] = g[h]
        return tuple(carries[0]), tuple(carries[1])

    key_pos = lax.broadcasted_iota(jnp.int32, (TK, TQ), 0)
    qry_pos = lax.broadcasted_iota(jnp.int32, (TK, TQ), 1)
    diag_mask = key_pos < qry_pos
    zero = jnp.zeros((1, TQ), F32)
    for h in range(first_parked, n_heads):
        o_ref[0, rows(h), cols] = jnp.zeros((HEAD_DIM, TQ), F32)
    carries = key_block(qi, ((zero,) * n_heads, (zero,) * n_heads), diag_mask, None)
    before = jnp.broadcast_to(qi > 0, (TK, TQ))
    carries = key_block(jnp.maximum(qi - 1, 0), carries, before, (qi, diag_mask))

    def any_weight_left(offs):
        top = functools.reduce(jnp.maximum, offs)
        return (jnp.max(jnp.exp2(top)) > 0.0).astype(jnp.int32)

    def more_blocks(state):
        i, alive, _, _ = state
        return jnp.logical_and(i < qi, alive > 0)

    def body(state):
        i, _, offs, used = state
        offs, used = key_block(qi - 1 - i, (offs, used), None, (qi - i, None))
        return i + 1, any_weight_left(offs), offs, used

    n_done, _, offs, used = lax.while_loop(
        more_blocks, body, (jnp.int32(1), any_weight_left(carries[0])) + carries)
    carries = (list(offs), list(used))
    last = jnp.maximum(qi - n_done, 0)
    g = {}
    for step in range(n_slots):
        stages(step, last, carries, None, ({}, g), (last, before))


def _attn_call(qT, k, vT, ntri):
    B, H, nq = qT.shape[:3]
    S = k.shape[1]
    return pl.pallas_call(
        functools.partial(_attn_kernel, n_heads=H),
        grid=(B, nq // Q_BLOCKS_PER_STEP),
        in_specs=[
            pl.BlockSpec((1, H, Q_BLOCKS_PER_STEP, 2 * HEAD_DIM, TQ), lambda b, q: (b, 0, q, 0, 0)),
            pl.BlockSpec((1, S, H * HEAD_DIM), lambda b, q: (b, 0, 0)),
            pl.BlockSpec((1, H, S // TK, HEAD_DIM, TK), lambda b, q: (b, 0, 0, 0, 0)),
            pl.BlockSpec((TK, TK), lambda b, q: (0, 0)),
        ],
        out_specs=pl.BlockSpec((1, H * HEAD_DIM, Q_BLOCKS_PER_STEP * TQ), lambda b, q: (b, 0, q)),
        out_shape=jax.ShapeDtypeStruct((B, H * HEAD_DIM, S), F32),
        scratch_shapes=[pltpu.VMEM((SKEW_SUMS + SKEW_VALUES - TAIL_STEPS, TK, TQ), F32),
                        pltpu.VMEM((SKEW_VALUES, TK, TQ), F32)],
        compiler_params=pltpu.CompilerParams(
            dimension_semantics=("parallel", "parallel"), vmem_limit_bytes=VMEM_LIMIT_BYTES),
        name="stickbreak_attn",
    )(qT, k, vT, ntri)


CONV_ROWS = 64
LANES = 128
SUBLANES = 8


def _mix_kernel(x_ref, xh_ref, attT_ref, wzT_ref, bzT_ref, wr_ref, br_ref, cw_ref, cb_ref, clg_ref, clb_ref,
                wap_ref, wcp_ref, bcp_ref, wo_ref, lg_ref, lb_ref, o_ref, cext_ref, shifted_ref, conv_ref, *,
                d_conv, alpha):
    i = pl.program_id(1)
    x = x_ref[0]
    xb = x.astype(BF16)
    tm, d_model = x.shape
    o_ga, o_gb, o_zc, o_gat, o_gcv, o_end = (0, d_conv, 2 * d_conv, 3 * d_conv, 3 * d_conv + d_model,
                                            3 * d_conv + 2 * d_model)

    def proj(lhs, lo, hi):
        return _dot(lhs, wr_ref[0, :, lo:hi]) + br_ref[0, :, lo:hi]

    xcat = jnp.concatenate([xh_ref[0].astype(BF16), xb], axis=0)
    half = d_conv // 2
    for lo in (0, half):
        a0, b0 = o_ga + lo, o_gb + lo
        c = proj(xcat, a0, a0 + half) * _sigmoid(proj(xcat, b0, b0 + half))
        cext_ref[0:HALO, lo:lo + half] = jnp.where(i > 0, c[0:HALO], 0.0)
        cext_ref[HALO:, lo:lo + half] = c[HALO:]

    gate_att = proj(xb, o_gat, o_gcv)
    z_conv = proj(xb, o_zc, o_gat)
    gate_conv = proj(xb, o_gcv, o_end)
    zaT = _dot_nt(wzT_ref[0], xb) + bzT_ref[0]

    lead = HALO - (CONV_WIDTH - 1)
    span = tm + HALO - SUBLANES
    for lb in range(d_conv // LANES):
        lanes = slice(lb * LANES, (lb + 1) * LANES)
        for r in range(1, SUBLANES):
            shifted_ref[r - 1, :, :] = cext_ref[r:r + span, lanes]
        for rc in range(tm // CONV_ROWS):
            acc = jnp.broadcast_to(cb_ref[0, :, lanes], (CONV_ROWS, LANES))
            for j in range(CONV_WIDTH):
                r, r0 = (j + lead) % SUBLANES, rc * CONV_ROWS + (j + lead) // SUBLANES * SUBLANES
                if r == 0:
                    taps = cext_ref[r0:r0 + CONV_ROWS, lanes]
                else:
                    taps = shifted_ref[r - 1, r0:r0 + CONV_ROWS, :]
                acc = acc + cw_ref[0, j:j + 1, lanes] * taps
            conv_ref[rc * CONV_ROWS:(rc + 1) * CONV_ROWS, lanes] = acc

    gT = attT_ref[0] * _silu(zaT)
    att_branch = _dot(gT.T.astype(BF16), wap_ref[0])
    cg = _silu(_layer_norm(conv_ref[...], clg_ref[0], clb_ref[0])) * _silu(z_conv)
    conv_branch = _dot(cg.astype(BF16), wcp_ref[0]) + bcp_ref[0]
    merged = _sigmoid(gate_att) * att_branch + _sigmoid(gate_conv) * conv_branch
    mb = merged.astype(BF16)
    for r0 in range(0, tm, tm // 2):
        r1 = r0 + tm // 2
        y = _dot(mb[r0:r1], wo_ref[0])
        o_ref[0, r0:r1, :] = _layer_norm(alpha * x[r0:r1] + y, lg_ref[0], lb_ref[0])


def _mix_call(x, attT, wzT, bzT, wr, br, cw, cb, clg, clb, wap, wcp, bcp, wo, lg, lb, layer, alpha):
    B, S, D = x.shape
    d_att = attT.shape[1]
    d_conv = cw.shape[-1]
    tm = TM_MIX
    lay = lambda b, i: (layer, 0, 0)
    full = lambda a: pl.BlockSpec((1,) + a.shape[1:], lay, pipeline_mode=pl.Buffered(1))
    return pl.pallas_call(
        functools.partial(_mix_kernel, d_conv=d_conv, alpha=alpha),
        grid=(B, S // tm),
        in_specs=[
            pl.BlockSpec((1, tm, D), lambda b, i: (b, i, 0)),
            pl.BlockSpec((1, HALO, D), lambda b, i: (b, jnp.maximum(i * (tm // HALO) - 1, 0), 0)),
            pl.BlockSpec((1, d_att, tm), lambda b, i: (b, 0, i)),
            full(wzT), full(bzT), full(wr), full(br), full(cw), full(cb), full(clg), full(clb),
            full(wap), full(wcp), full(bcp), full(wo), full(lg), full(lb),
        ],
        out_specs=pl.BlockSpec((1, tm, D), lambda b, i: (b, i, 0)),
        out_shape=jax.ShapeDtypeStruct((B, S, D), F32),
        scratch_shapes=[pltpu.VMEM((HALO + tm, d_conv), F32),
                        pltpu.VMEM((SUBLANES - 1, HALO + tm - SUBLANES, LANES), F32),
                        pltpu.VMEM((tm, d_conv), F32)],
        compiler_params=pltpu.CompilerParams(
            dimension_semantics=("parallel", "parallel"), vmem_limit_bytes=VMEM_LIMIT_BYTES),
        name="mix_out_norm",
    )(x, x, attT, wzT, bzT, wr, br, cw, cb, clg, clb, wap, wcp, bcp, wo, lg, lb)


def kernel(x, w_in, b_in, conv_w, conv_b, conv_ln_g, conv_ln_b, w_att_proj, w_conv_proj, b_conv_proj, w_out,
           ln_g, ln_b):
    B, S, D = x.shape
    depth = w_in.shape[0]
    d_att = w_att_proj.shape[1]
    d_conv = conv_w.shape[-1]
    n_heads = d_att // HEAD_DIM
    assert S % TM_QKV == 0 and S % TM_MIX == 0 and TM_QKV % TQ == 0 and TQ == TK
    assert (S // TQ) % Q_BLOCKS_PER_STEP == 0
    assert n_heads % 2 == 0 and d_conv == 4 * LANES and TM_MIX % CONV_ROWS == 0
    assert n_heads >= SKEW_SUMS + SKEW_VALUES and 0 <= TAIL_STEPS <= SKEW_SUMS and CONV_WIDTH - 1 <= HALO
    assert w_in.shape[-1] == 4 * d_att + 3 * d_conv + 2 * D
    alpha = (2 * depth) ** 0.25

    o_k, o_v, o_z, o_r = d_att, 2 * d_att, 3 * d_att, 4 * d_att
    wk = w_in[:, :, o_k:o_v].astype(BF16)
    bk = b_in[:, None, o_k:o_v]
    wqvT = jnp.swapaxes(jnp.concatenate([w_in[:, :, :o_k], w_in[:, :, o_v:o_z]], axis=-1), 1, 2).astype(BF16)
    bqvT = jnp.concatenate([b_in[:, :o_k], b_in[:, o_v:o_z]], axis=-1)[:, :, None]
    wzT = jnp.swapaxes(w_in[:, :, o_z:o_r], 1, 2).astype(BF16)
    bzT = b_in[:, o_z:o_r, None]
    wr = w_in[:, :, o_r:].astype(BF16)
    br = b_in[:, None, o_r:]
    row = lambda a: a[:, None, :]
    wap, wcp, wo = w_att_proj.astype(BF16), w_conv_proj.astype(BF16), w_out.astype(BF16)

    r = lax.broadcasted_iota(jnp.int32, (TK, TK), 0)
    c = lax.broadcasted_iota(jnp.int32, (TK, TK), 1)
    ntri = jnp.where(c >= r, -1.0, 0.0).astype(BF16)

    for l in range(depth):
        k, qT, vT = _qkv_call(x, wk, bk, wqvT, bqvT, l, n_heads)
        attT = _attn_call(qT, k, vT, ntri)
        x = _mix_call(x, attT, wzT, bzT, wr, br, conv_w, row(conv_b), row(conv_ln_g), row(conv_ln_b),
                      wap, wcp, row(b_conv_proj), wo, row(ln_g), row(ln_b), l, alpha)
    return x
```

```python
import functools
import math

import jax
import jax.numpy as jnp
from jax import lax
from jax.experimental import pallas as pl
from jax.experimental.pallas import tpu as pltpu

F32 = jnp.float32
BF16 = jnp.bfloat16

HEAD_DIM = 64
CONV_WIDTH = 31
LN_EPS = 1e-5
LOG2E = math.log2(math.e)

TQ = 256
TK = 256
SKEW_SUMS = 3
SKEW_VALUES = 2
TAIL_STEPS = 3
HALO = 32
TM_QKV = 1024
Q_BLOCKS_PER_STEP = 4
TM_MIX = 512
VMEM_LIMIT_BYTES = 56 * 1024 * 1024

_NT = (((1,), (1,)), ((), ()))


def _dot(a, b):
    return jnp.dot(a, b, preferred_element_type=F32)


def _dot_nt(a, b):
    return lax.dot_general(a, b, _NT, preferred_element_type=F32)


def _sigmoid(v):
    return 0.5 * jnp.tanh(0.5 * v) + 0.5


def _silu(v):
    return v * _sigmoid(v)


def _layer_norm(v, g, b):
    mu = jnp.mean(v, axis=-1, keepdims=True)
    d = v - mu
    var = jnp.mean(d * d, axis=-1, keepdims=True)
    return d * lax.rsqrt(var + LN_EPS) * g + b


def _qkv_kernel(x_ref, wk_ref, bk_ref, wqvT_ref, bqvT_ref, k_ref, qT_ref, vT_ref, *, n_heads, q_scale):
    xb = x_ref[0].astype(BF16)
    k = _dot(xb, wk_ref[0]) + bk_ref[0]
    k_ref[0] = k.astype(BF16)
    qvT = _dot_nt(wqvT_ref[0], xb) + bqvT_ref[0]
    d_att = n_heads * HEAD_DIM
    tm = xb.shape[0]
    zeros = jnp.zeros((HEAD_DIM, TQ), BF16)
    for h in range(n_heads):
        rows = slice(h * HEAD_DIM, (h + 1) * HEAD_DIM)
        vrows = slice(d_att + h * HEAD_DIM, d_att + (h + 1) * HEAD_DIM)
        lo = (h % 2) * HEAD_DIM
        other = HEAD_DIM - lo
        for c in range(tm // TQ):
            cols = slice(c * TQ, (c + 1) * TQ)
            qT_ref[0, h, c, lo:lo + HEAD_DIM, :] = (qvT[rows, cols] * q_scale).astype(BF16)
            qT_ref[0, h, c, other:other + HEAD_DIM, :] = zeros
            vT_ref[0, h, c] = qvT[vrows, cols].astype(BF16)


def _qkv_call(x, wk, bk, wqvT, bqvT, layer, n_heads):
    B, S, D = x.shape
    d_att = n_heads * HEAD_DIM
    tm = TM_QKV
    nq = S // TQ
    lay = lambda b, i: (layer, 0, 0)
    return pl.pallas_call(
        functools.partial(_qkv_kernel, n_heads=n_heads, q_scale=HEAD_DIM ** -0.5 * LOG2E),
        grid=(B, S // tm),
        in_specs=[
            pl.BlockSpec((1, tm, D), lambda b, i: (b, i, 0)),
            pl.BlockSpec((1, D, d_att), lay),
            pl.BlockSpec((1, 1, d_att), lay),
            pl.BlockSpec((1, 2 * d_att, D), lay),
            pl.BlockSpec((1, 2 * d_att, 1), lay),
        ],
        out_specs=[
            pl.BlockSpec((1, tm, d_att), lambda b, i: (b, i, 0)),
            pl.BlockSpec((1, n_heads, tm // TQ, 2 * HEAD_DIM, TQ), lambda b, i: (b, 0, i, 0, 0)),
            pl.BlockSpec((1, n_heads, tm // TK, HEAD_DIM, TK), lambda b, i: (b, 0, i, 0, 0)),
        ],
        out_shape=[
            jax.ShapeDtypeStruct((B, S, d_att), BF16),
            jax.ShapeDtypeStruct((B, n_heads, nq, 2 * HEAD_DIM, TQ), BF16),
            jax.ShapeDtypeStruct((B, n_heads, S // TK, HEAD_DIM, TK), BF16),
        ],
        compiler_params=pltpu.CompilerParams(
            dimension_semantics=("parallel", "parallel"), vmem_limit_bytes=VMEM_LIMIT_BYTES),
        name="qkv_proj",
    )(x, wk, bk, wqvT, bqvT)


def _attn_kernel(qT_ref, k_ref, vT_ref, ntri_ref, o_ref, zbuf_ref, gbuf_ref, *, n_heads):
    for sub in range(Q_BLOCKS_PER_STEP):
        _attn_query_block(pl.program_id(1) * Q_BLOCKS_PER_STEP + sub, sub, qT_ref, k_ref, vT_ref, ntri_ref,
                          o_ref, zbuf_ref, gbuf_ref, n_heads)


def _attn_query_block(qi, sub, qT_ref, k_ref, vT_ref, ntri_ref, o_ref, zbuf_ref, gbuf_ref, n_heads):
    cols = slice(sub * TQ, (sub + 1) * TQ)
    ntri = ntri_ref[...]
    sign = jnp.uint32(0x80000000)

    def scores(h, kb):
        pair = slice((h // 2) * 2 * HEAD_DIM, (h // 2 + 1) * 2 * HEAD_DIM)
        kblk = k_ref[0, pl.ds(pl.multiple_of(kb * TK, TK), TK), pair]
        return _dot(kblk, qT_ref[0, h, sub])

    def log_fail_sums(z, mask):
        neg_abs = lax.bitcast_convert_type(lax.bitcast_convert_type(z, jnp.uint32) | sign, F32)
        p = jnp.maximum(z, 0.0) + jnp.log(1.0 + jnp.exp2(neg_abs)) * LOG2E
        if mask is not None:
            p = jnp.where(mask, p, 0.0)
        return _dot(ntri, p.astype(BF16))

    def weighted_values(h, kb, z, g, offset, mask):
        w = jnp.exp2(z + g)
        if mask is not None:
            w = jnp.where(mask, w, 0.0)
        return _dot(vT_ref[0, h, kb], w.astype(BF16)) * jnp.exp2(offset)

    def rows(h):
        return slice(h * HEAD_DIM, (h + 1) * HEAD_DIM)

    d1, d2, tail = SKEW_SUMS, SKEW_VALUES, TAIL_STEPS
    n_steps = n_heads + tail
    first_parked = n_steps - d1 - d2
    n_slots = n_heads - first_parked

    def stages(step, kb, carries, mask, cur, prev):
        z, g = cur
        offs, used = carries
        h = step - d1
        if h in z:
            g[h] = log_fail_sums(z[h], mask)
            used[h], offs[h] = offs[h], offs[h] + g[h][0:1, :]
        elif prev is not None and first_parked <= n_steps + h < n_heads:
            hp = n_steps + h
            g[h] = log_fail_sums(zbuf_ref[hp - first_parked], prev[1])
            used[hp], offs[hp] = offs[hp], offs[hp] + g[h][0:1, :]
        h = step - d1 - d2
        if h in z:
            pv = weighted_values(h, kb, z.pop(h), g.pop(h), used[h], mask)
            if prev is None:
                o_ref[0, rows(h), cols] = pv
            else:
                o_ref[0, rows(h), cols] += pv
        elif prev is not None and first_parked <= n_steps + h < n_heads:
            hp = n_steps + h
            slot = hp - first_parked
            gp = g.pop(h) if h in g else gbuf_ref[slot]
            o_ref[0, rows(hp), cols] += weighted_values(hp, prev[0], zbuf_ref[slot], gp, used[hp], prev[1])

    def key_block(kb, carries, mask, prev):
        carries = (list(carries[0]), list(carries[1]))
        z, g = {}, {}
        for step in range(n_steps):
            if step < n_heads:
                z[step] = scores(step, kb)
            stages(step, kb, carries, mask, (z, g), prev)
        for slot in range(n_slots):
            h = first_parked + slot
            zbuf_ref[slot] = z[h]
            if h in g:
                gbuf_ref[slot] = g[h]
        return tuple(carries[0]), tuple(carries[1])

    key_pos = lax.broadcasted_iota(jnp.int32, (TK, TQ), 0)
    qry_pos = lax.broadcasted_iota(jnp.int32, (TK, TQ), 1)
    diag_mask = key_pos < qry_pos
    zero = jnp.zeros((1, TQ), F32)
    for h in range(first_parked, n_heads):
        o_ref[0, rows(h), cols] = jnp.zeros((HEAD_DIM, TQ), F32)
    carries = key_block(qi, ((zero,) * n_heads, (zero,) * n_heads), diag_mask, None)
    before = jnp.broadcast_to(qi > 0, (TK, TQ))
    carries = key_block(jnp.maximum(qi - 1, 0), carries, before, (qi, diag_mask))

    def any_weight_left(offs):
        top = functools.reduce(jnp.maximum, offs)
        return (jnp.max(jnp.exp2(top)) > 0.0).astype(jnp.int32)

    def more_blocks(state):
        i, alive, _, _ = state
        return jnp.logical_and(i < qi, alive > 0)

    def body(state):
        i, _, offs, used = state
        offs, used = key_block(qi - 1 - i, (offs, used), None, (qi - i, None))
        return i + 1, any_weight_left(offs), offs, used

    n_done, _, offs, used = lax.while_loop(
        more_blocks, body, (jnp.int32(1), any_weight_left(carries[0])) + carries)
    carries = (list(offs), list(used))
    last = jnp.maximum(qi - n_done, 0)
    g = {}
    for step in range(n_slots):
        stages(step, last, carries, None, ({}, g), (last, before))


def _attn_call(qT, k, vT, ntri):
    B, H, nq = qT.shape[:3]
    S = k.shape[1]
    return pl.pallas_call(
        functools.partial(_attn_kernel, n_heads=H),
        grid=(B, nq // Q_BLOCKS_PER_STEP),
        in_specs=[
            pl.BlockSpec((1, H, Q_BLOCKS_PER_STEP, 2 * HEAD_DIM, TQ), lambda b, q: (b, 0, q, 0, 0)),
            pl.BlockSpec((1, S, H * HEAD_DIM), lambda b, q: (b, 0, 0)),
            pl.BlockSpec((1, H, S // TK, HEAD_DIM, TK), lambda b, q: (b, 0, 0, 0, 0)),
            pl.BlockSpec((TK, TK), lambda b, q: (0, 0)),
        ],
        out_specs=pl.BlockSpec((1, H * HEAD_DIM, Q_BLOCKS_PER_STEP * TQ), lambda b, q: (b, 0, q)),
        out_shape=jax.ShapeDtypeStruct((B, H * HEAD_DIM, S), F32),
        scratch_shapes=[pltpu.VMEM((SKEW_SUMS + SKEW_VALUES - TAIL_STEPS, TK, TQ), F32),
                        pltpu.VMEM((SKEW_VALUES, TK, TQ), F32)],
        compiler_params=pltpu.CompilerParams(
            dimension_semantics=("parallel", "parallel"), vmem_limit_bytes=VMEM_LIMIT_BYTES),
        name="stickbreak_attn",
    )(qT, k, vT, ntri)


CONV_ROWS = 64
LANES = 128
SUBLANES = 8


def _mix_kernel(x_ref, xh_ref, attT_ref, wzT_ref, bzT_ref, wr_ref, br_ref, cw_ref, cb_ref, clg_ref, clb_ref,
                wap_ref, wcp_ref, bcp_ref, wo_ref, lg_ref, lb_ref, o_ref, cext_ref, shifted_ref, conv_ref, *,
                d_conv, alpha):
    i = pl.program_id(1)
    x = x_ref[0]
    xb = x.astype(BF16)
    tm, d_model = x.shape
    o_ga, o_gb, o_zc, o_gat, o_gcv, o_end = (0, d_conv, 2 * d_conv, 3 * d_conv, 3 * d_conv + d_model,
                                            3 * d_conv + 2 * d_model)

    def proj(lhs, lo, hi):
        return _dot(lhs, wr_ref[0, :, lo:hi]) + br_ref[0, :, lo:hi]

    xcat = jnp.concatenate([xh_ref[0].astype(BF16), xb], axis=0)
    half = d_conv // 2
    for lo in (0, half):
        a0, b0 = o_ga + lo, o_gb + lo
        c = proj(xcat, a0, a0 + half) * _sigmoid(proj(xcat, b0, b0 + half))
        cext_ref[0:HALO, lo:lo + half] = jnp.where(i > 0, c[0:HALO], 0.0)
        cext_ref[HALO:, lo:lo + half] = c[HALO:]

    gate_att = proj(xb, o_gat, o_gcv)
    z_conv = proj(xb, o_zc, o_gat)
    gate_conv = proj(xb, o_gcv, o_end)
    zaT = _dot_nt(wzT_ref[0], xb) + bzT_ref[0]

    lead = HALO - (CONV_WIDTH - 1)
    span = tm + HALO - SUBLANES
    for lb in range(d_conv // LANES):
        lanes = slice(lb * LANES, (lb + 1) * LANES)
        for r in range(1, SUBLANES):
            shifted_ref[r - 1, :, :] = cext_ref[r:r + span, lanes]
        for rc in range(tm // CONV_ROWS):
            acc = jnp.broadcast_to(cb_ref[0, :, lanes], (CONV_ROWS, LANES))
            for j in range(CONV_WIDTH):
                r, r0 = (j + lead) % SUBLANES, rc * CONV_ROWS + (j + lead) // SUBLANES * SUBLANES
                if r == 0:
                    taps = cext_ref[r0:r0 + CONV_ROWS, lanes]
                else:
                    taps = shifted_ref[r - 1, r0:r0 + CONV_ROWS, :]
                acc = acc + cw_ref[0, j:j + 1, lanes] * taps
            conv_ref[rc * CONV_ROWS:(rc + 1) * CONV_ROWS, lanes] = acc

    gT = attT_ref[0] * _silu(zaT)
    att_branch = _dot(gT.T.astype(BF16), wap_ref[0])
    cg = _silu(_layer_norm(conv_ref[...], clg_ref[0], clb_ref[0])) * _silu(z_conv)
    conv_branch = _dot(cg.astype(BF16), wcp_ref[0]) + bcp_ref[0]
    merged = _sigmoid(gate_att) * att_branch + _sigmoid(gate_conv) * conv_branch
    mb = merged.astype(BF16)
    for r0 in range(0, tm, tm // 2):
        r1 = r0 + tm // 2
        y = _dot(mb[r0:r1], wo_ref[0])
        o_ref[0, r0:r1, :] = _layer_norm(alpha * x[r0:r1] + y, lg_ref[0], lb_ref[0])


def _mix_call(x, attT, wzT, bzT, wr, br, cw, cb, clg, clb, wap, wcp, bcp, wo, lg, lb, layer, alpha):
    B, S, D = x.shape
    d_att = attT.shape[1]
    d_conv = cw.shape[-1]
    tm = TM_MIX
    lay = lambda b, i: (layer, 0, 0)
    full = lambda a: pl.BlockSpec((1,) + a.shape[1:], lay, pipeline_mode=pl.Buffered(1))
    return pl.pallas_call(
        functools.partial(_mix_kernel, d_conv=d_conv, alpha=alpha),
        grid=(B, S // tm),
        in_specs=[
            pl.BlockSpec((1, tm, D), lambda b, i: (b, i, 0)),
            pl.BlockSpec((1, HALO, D), lambda b, i: (b, jnp.maximum(i * (tm // HALO) - 1, 0), 0)),
            pl.BlockSpec((1, d_att, tm), lambda b, i: (b, 0, i)),
            full(wzT), full(bzT), full(wr), full(br), full(cw), full(cb), full(clg), full(clb),
            full(wap), full(wcp), full(bcp), full(wo), full(lg), full(lb),
        ],
        out_specs=pl.BlockSpec((1, tm, D), lambda b, i: (b, i, 0)),
        out_shape=jax.ShapeDtypeStruct((B, S, D), F32),
        scratch_shapes=[pltpu.VMEM((HALO + tm, d_conv), F32),
                        pltpu.VMEM((SUBLANES - 1, HALO + tm - SUBLANES, LANES), F32),
                        pltpu.VMEM((tm, d_conv), F32)],
        compiler_params=pltpu.CompilerParams(
            dimension_semantics=("parallel", "parallel"), vmem_limit_bytes=VMEM_LIMIT_BYTES),
        name="mix_out_norm",
    )(x, x, attT, wzT, bzT, wr, br, cw, cb, clg, clb, wap, wcp, bcp, wo, lg, lb)


def kernel(x, w_in, b_in, conv_w, conv_b, conv_ln_g, conv_ln_b, w_att_proj, w_conv_proj, b_conv_proj, w_out,
           ln_g, ln_b):
    B, S, D = x.shape
    depth = w_in.shape[0]
    d_att = w_att_proj.shape[1]
    d_conv = conv_w.shape[-1]
    n_heads = d_att // HEAD_DIM
    assert S % TM_QKV == 0 and S % TM_MIX == 0 and TM_QKV % TQ == 0 and TQ == TK
    assert (S // TQ) % Q_BLOCKS_PER_STEP == 0
    assert n_heads % 2 == 0 and d_conv == 4 * LANES and TM_MIX % CONV_ROWS == 0
    assert n_heads >= SKEW_SUMS + SKEW_VALUES and 0 <= TAIL_STEPS <= SKEW_SUMS and CONV_WIDTH - 1 <= HALO
    assert w_in.shape[-1] == 4 * d_att + 3 * d_conv + 2 * D
    alpha = (2 * depth) ** 0.25

    o_k, o_v, o_z, o_r = d_att, 2 * d_att, 3 * d_att, 4 * d_att
    wk = w_in[:, :, o_k:o_v].astype(BF16)
    bk = b_in[:, None, o_k:o_v]
    wqvT = jnp.swapaxes(jnp.concatenate([w_in[:, :, :o_k], w_in[:, :, o_v:o_z]], axis=-1), 1, 2).astype(BF16)
    bqvT = jnp.concatenate([b_in[:, :o_k], b_in[:, o_v:o_z]], axis=-1)[:, :, None]
    wzT = jnp.swapaxes(w_in[:, :, o_z:o_r], 1, 2).astype(BF16)
    bzT = b_in[:, o_z:o_r, None]
    wr = w_in[:, :, o_r:].astype(BF16)
    br = b_in[:, None, o_r:]
    row = lambda a: a[:, None, :]
    wap, wcp, wo = w_att_proj.astype(BF16), w_conv_proj.astype(BF16), w_out.astype(BF16)

    r = lax.broadcasted_iota(jnp.int32, (TK, TK), 0)
    c = lax.broadcasted_iota(jnp.int32, (TK, TK), 1)
    ntri = jnp.where(c >= r, -1.0, 0.0).astype(BF16)

    for l in range(depth):
        k, qT, vT = _qkv_call(x, wk, bk, wqvT, bqvT, l, n_heads)
        attT = _attn_call(qT, k, vT, ntri)
        x = _mix_call(x, attT, wzT, bzT, wr, br, conv_w, row(conv_b), row(conv_ln_g), row(conv_ln_b),
                      wap, wcp, row(b_conv_proj), wo, row(ln_g), row(ln_b), l, alpha)
    return x
```
